```python
import jax, jax.numpy as jnp
from jax import lax
import numpy as np

D_MODEL = 1024
BATCH = 1
SEQ = 16384
DEPTH = 4

HEAD_DIM = 64
N_MIXERS = 3
NORM_EPS = 1e-6
GRID_W = 64
A_HEADS = 16
A_KV = 4
A_G = A_HEADS // A_KV
A_WINDOW = 128
A_BLOCK = 128
B_HEADS = 16
B_KV = 4
B_G = B_HEADS // B_KV
B_BLOCK = 128
ROPE_THETA = 10000.0
C_GROUPS = ((128, 1), (512, 4), (2048, 16))
C_NGROUPS = len(C_GROUPS)
C_HEADS_PER_GROUP = 8
C_KV = 2
C_G = C_HEADS_PER_GROUP // C_KV
C_HEADS = C_NGROUPS * C_HEADS_PER_GROUP
C_BLOCK = max(w // d // 2 for w, d in C_GROUPS)

A_QW = A_HEADS * HEAD_DIM
A_KVW = A_KV * HEAD_DIM
A_OUT = A_QW
B_QW = B_HEADS * HEAD_DIM
B_KVW = B_KV * HEAD_DIM
B_OUT = B_QW
C_QW = C_HEADS * HEAD_DIM
C_KVW = C_NGROUPS * C_KV * HEAD_DIM
C_OUT = C_HEADS_PER_GROUP * HEAD_DIM
IN_WIDTH = (A_QW + 2 * A_KVW + A_OUT, B_QW + 2 * B_KVW + B_OUT, C_QW + 2 * C_KVW + C_OUT)
OUT_WIDTH = (A_OUT, B_OUT, C_OUT)

kernel_name = "hybrid_interleaved_bidir_attn_encoder"


def rms_norm(x, gain):
    xf = x.astype(jnp.float32)
    y = xf * lax.rsqrt(jnp.mean(xf * xf, axis=-1, keepdims=True) + NORM_EPS)
    return (y * gain.astype(jnp.float32)).astype(x.dtype)


def alibi_slopes(n):
    return jnp.asarray(2.0 ** (-8.0 * np.arange(1, n + 1) / n), dtype=jnp.float32)


def banded_attention(q, k, v, half, block, slopes, dist_scale, sink, length):
    n, L, kvh, g, dh = q.shape
    nb = L // block
    qb = q.reshape(n, nb, block, kvh, g, dh).astype(jnp.float32)

    def windows(t):
        tp = jnp.pad(t, ((0, 0), (block, block), (0, 0), (0, 0))).reshape(n, nb + 2, block, kvh, dh)
        return jnp.concatenate([tp[:, :-2], tp[:, 1:-1], tp[:, 2:]], axis=2).astype(jnp.float32)

    kw, vw = windows(k), windows(v)
    s = jnp.einsum("nbqkgd,nbckd->nbkgqc", qb, kw) * (dh ** -0.5)
    a = jnp.arange(block)[:, None]
    c = jnp.arange(3 * block)[None, :]
    rel = c - block - a
    kpos = jnp.arange(nb)[:, None, None] * block - block + c
    valid = (jnp.abs(rel) <= half) & (((kpos >= 0) & (kpos < length)) | (rel == 0))
    bias = -(slopes * dist_scale)[:, :, None, None] * jnp.abs(rel).astype(jnp.float32)
    s = jnp.where(valid[None, :, None, None], s + bias, -jnp.inf)
    m = jnp.max(s, axis=-1, keepdims=True)
    if sink is not None:
        sk = sink.astype(jnp.float32)[:, :, None, None]
        m = jnp.maximum(m, sk)
    p = jnp.exp(s - m)
    den = jnp.sum(p, axis=-1, keepdims=True)
    if sink is not None:
        den = den + jnp.exp(sk - m)
    o = jnp.einsum("nbkgqc,nbckd->nbqkgd", p / den, vw).reshape(n, L, kvh, g, dh)
    lse = jnp.moveaxis((m + jnp.log(den))[..., 0], 4, 2).reshape(n, L, kvh, g)
    return o.astype(q.dtype), lse


def mixer_a(h, w_in, q_gain, k_gain, sink, w_out):
    B, S, _ = h.shape
    q, k, v, gate = jnp.split(h @ w_in, [A_QW, A_QW + A_KVW, A_QW + 2 * A_KVW], axis=-1)
    q = rms_norm(q.reshape(B, S, A_KV, A_G, HEAD_DIM), q_gain)
    k = rms_norm(k.reshape(B, S, A_KV, HEAD_DIM), k_gain)
    v = v.reshape(B, S, A_KV, HEAD_DIM)
    slopes = alibi_slopes(A_HEADS).reshape(A_KV, A_G)
    o, _ = banded_attention(q, k, v, A_WINDOW, A_BLOCK, slopes, 1.0, sink.reshape(A_KV, A_G), S)
    return (o.reshape(B, S, A_OUT) * jax.nn.silu(gate)) @ w_out


def axial_rope_tables(S):
    rows = S // GRID_W
    row = jnp.repeat(jnp.arange(rows), GRID_W).astype(jnp.float32)
    col = jnp.tile(jnp.arange(GRID_W), rows).astype(jnp.float32)
    axis_dim = HEAD_DIM // 2
    freqs = 1.0 / (ROPE_THETA ** (jnp.arange(0, axis_dim, 2, dtype=jnp.float32) / axis_dim))
    ang = jnp.concatenate([row[:, None] * freqs, col[:, None] * freqs], axis=-1)
    return jnp.cos(ang), jnp.sin(ang)


def apply_rope(x, cos, sin):
    xf = x.astype(jnp.float32).reshape(*x.shape[:-1], HEAD_DIM // 2, 2)
    x1, x2 = xf[..., 0], xf[..., 1]
    out = jnp.stack([x1 * cos - x2 * sin, x1 * sin + x2 * cos], axis=-1)
    return out.reshape(x.shape).astype(x.dtype)


def dense_block_attention(q, k, v):
    B, S, kvh, g, dh = q.shape
    nb = S // B_BLOCK
    qb = jnp.moveaxis(q.reshape(B, nb, B_BLOCK, kvh, g, dh), 1, 0)
    kf, vf = k.astype(jnp.float32), v.astype(jnp.float32)

    def one_block(qblk):
        s = jnp.einsum("bqkgd,bskd->bkgqs", qblk.astype(jnp.float32), kf) * (dh ** -0.5)
        return jnp.einsum("bkgqs,bskd->bqkgd", jax.nn.softmax(s, axis=-1), vf)

    o = lax.map(one_block, qb)
    return jnp.moveaxis(o, 0, 1).reshape(B, S, kvh, g, dh).astype(q.dtype)


def mixer_b(h, w_in, q_gain, k_gain, w_out):
    B, S, _ = h.shape
    q, k, v, gate = jnp.split(h @ w_in, [B_QW, B_QW + B_KVW, B_QW + 2 * B_KVW], axis=-1)
    q = rms_norm(q.reshape(B, S, B_KV, B_G, HEAD_DIM), q_gain)
    k = rms_norm(k.reshape(B, S, B_KV, HEAD_DIM), k_gain)
    v = v.reshape(B, S, B_KV, HEAD_DIM)
    cos, sin = axial_rope_tables(S)
    q = apply_rope(q, cos[None, :, None, None, :], sin[None, :, None, None, :])
    k = apply_rope(k, cos[None, :, None, :], sin[None, :, None, :])
    o = dense_block_attention(q, k, v)
    return (o.reshape(B, S, B_OUT) * jax.nn.silu(gate)) @ w_out


def to_strided(t, dil, L, Lp):
    B = t.shape[0]
    rest = t.shape[2:]
    t = jnp.moveaxis(t.reshape(B, L, dil, *rest), 2, 1).reshape(B * dil, L, *rest)
    return jnp.pad(t, ((0, 0), (0, Lp - L)) + ((0, 0),) * len(rest))


def from_strided(t, B, dil, L):
    rest = t.shape[2:]
    t = t[:, :L].reshape(B, dil, L, *rest)
    return jnp.moveaxis(t, 1, 2).reshape(B, L * dil, *rest)


def mixer_c(h, w_in, q_gain, k_gain, w_out):
    B, S, _ = h.shape
    q, k, v, gate = jnp.split(h @ w_in, [C_QW, C_QW + C_KVW, C_QW + 2 * C_KVW], axis=-1)
    q = rms_norm(q.reshape(B, S, C_NGROUPS, C_KV, C_G, HEAD_DIM), q_gain)
    k = rms_norm(k.reshape(B, S, C_NGROUPS, C_KV, HEAD_DIM), k_gain)
    v = v.reshape(B, S, C_NGROUPS, C_KV, HEAD_DIM)
    slopes = alibi_slopes(C_HEADS).reshape(C_NGROUPS, C_KV, C_G)
    outs, lses = [], []
    for gi, (window, dil) in enumerate(C_GROUPS):
        half = window // dil // 2
        L = S // dil
        Lp = -(-L // C_BLOCK) * C_BLOCK
        qg = to_strided(q[:, :, gi], dil, L, Lp)
        kg = to_strided(k[:, :, gi], dil, L, Lp)
        vg = to_strided(v[:, :, gi], dil, L, Lp)
        o, lse = banded_attention(qg, kg, vg, half, C_BLOCK, slopes[gi], float(dil), None, L)
        outs.append(from_strided(o, B, dil, L))
        lses.append(from_strided(lse, B, dil, L))
    w = jax.nn.softmax(jnp.stack(lses, axis=0), axis=0)
    o = jnp.sum(w[..., None] * jnp.stack(outs, axis=0).astype(jnp.float32), axis=0).astype(h.dtype)
    return (o.reshape(B, S, C_OUT) * jax.nn.silu(gate)) @ w_out


def setup_inputs(seed: int = 0) -> dict:
    key = jax.random.key(seed)
    keys = jax.random.split(key, 1 + 6 * DEPTH)
    out = {"x": jax.random.normal(keys[0], (BATCH, SEQ, D_MODEL), jnp.float32)}
    for i in range(DEPTH):
        kind = i % N_MIXERS
        k = keys[1 + 6 * i: 7 + 6 * i]
        in_w, out_w = IN_WIDTH[kind], OUT_WIDTH[kind]
        out[f"l{i}_norm"] = 1.0 + 0.02 * jax.random.normal(k[0], (D_MODEL,), jnp.float32)
        out[f"l{i}_w_in"] = jax.random.normal(k[1], (D_MODEL, in_w), jnp.float32) * D_MODEL ** -0.5
        out[f"l{i}_q_gain"] = 1.0 + 0.02 * jax.random.normal(k[2], (HEAD_DIM,), jnp.float32)
        out[f"l{i}_k_gain"] = 1.0 + 0.02 * jax.random.normal(k[3], (HEAD_DIM,), jnp.float32)
        if kind == 0:
            out[f"l{i}_sink"] = 0.5 * jax.random.normal(k[4], (A_HEADS,), jnp.float32)
        out[f"l{i}_w_out"] = jax.random.normal(k[5], (out_w, D_MODEL), jnp.float32) * out_w ** -0.5
    return out


def reference(x, l0_norm, l0_w_in, l0_q_gain, l0_k_gain, l0_sink, l0_w_out,
              l1_norm, l1_w_in, l1_q_gain, l1_k_gain, l1_w_out,
              l2_norm, l2_w_in, l2_q_gain, l2_k_gain, l2_w_out,
              l3_norm, l3_w_in, l3_q_gain, l3_k_gain, l3_sink, l3_w_out):
    mixers = (mixer_a, mixer_b, mixer_c)
    layers = (
        (l0_norm, (l0_w_in, l0_q_gain, l0_k_gain, l0_sink, l0_w_out)),
        (l1_norm, (l1_w_in, l1_q_gain, l1_k_gain, l1_w_out)),
        (l2_norm, (l2_w_in, l2_q_gain, l2_k_gain, l2_w_out)),
        (l3_norm, (l3_w_in, l3_q_gain, l3_k_gain, l3_sink, l3_w_out)),
    )
    for i in range(DEPTH):
        norm, params = layers[i]
        x = x + mixers[i % N_MIXERS](rms_norm(x, norm), *params)
    return x
```

```python
import functools

import numpy as np
import jax
import jax.numpy as jnp
from jax import lax
from jax.experimental import pallas as pl
from jax.experimental.pallas import tpu as pltpu

D_MODEL = 1024
SEQ = 16384
HEAD_DIM = 64
NORM_EPS = 1e-6
GRID_W = 64
ROPE_THETA = 10000.0
A_WINDOW = 128
C_GROUPS = ((128, 1), (512, 4), (2048, 16))

LANES = 128
HEADS_PER_PAIR = 8
PAIR_ROWS = HEADS_PER_PAIR * HEAD_DIM
VMEM_LIMIT = 56 * 1024 * 1024

BF16 = jnp.bfloat16
F32 = jnp.float32


def _params(*sem):
    return pltpu.CompilerParams(dimension_semantics=sem, vmem_limit_bytes=VMEM_LIMIT)


def _transpose_kernel(x_ref, o_ref):
    o_ref[...] = x_ref[...].T


def _to_feature_major(x2d, tm=512):
    return pl.pallas_call(
        _transpose_kernel,
        grid=(SEQ // tm,),
        in_specs=[pl.BlockSpec((tm, D_MODEL), lambda i: (i, 0))],
        out_specs=pl.BlockSpec((D_MODEL, tm), lambda i: (0, i)),
        out_shape=jax.ShapeDtypeStruct((D_MODEL, SEQ), F32),
        compiler_params=_params("parallel"),
        name="to_feature_major",
    )(x2d)


def _inproj_kernel(*refs, nq, nkv, gate_w, rope, tm):
    if rope:
        (xT_ref, ng_ref, w_ref, qg_ref, kg_ref, cos_ref, sin_ref,
         qT_ref, k_ref, vT_ref, sgT_ref) = refs
        cos, sin = cos_ref[...], sin_ref[...]
    else:
        xT_ref, ng_ref, w_ref, qg_ref, kg_ref, qT_ref, k_ref, vT_ref, sgT_ref = refs
    rep = tm // LANES
    x = xT_ref[...]
    r = lax.rsqrt(jnp.mean(x * x, axis=0, keepdims=True) + NORM_EPS)
    h = (x * r * pltpu.repeat(ng_ref[...], rep, axis=1)).astype(BF16)
    qg = pltpu.repeat(qg_ref[...], rep, axis=1)
    kg = pltpu.repeat(kg_ref[...], rep, axis=1)

    def head_norm(ph, gain, scale):
        ss = jnp.sum(ph * ph, axis=0, keepdims=True)
        y = ph * (lax.rsqrt(ss * (1.0 / HEAD_DIM) + NORM_EPS) * scale) * gain
        if rope:
            half = HEAD_DIM // 2
            x1, x2 = y[:half], y[half:]
            y = jnp.concatenate([x1 * cos - x2 * sin, x1 * sin + x2 * cos], axis=0)
        return y

    qw, kw = nq * HEAD_DIM, nkv * HEAD_DIM
    chunk = 256
    for c0 in range(0, qw, chunk):
        pc = jnp.dot(w_ref[c0:c0 + chunk, :], h, preferred_element_type=F32)
        for j in range(chunk // HEAD_DIM):
            y = head_norm(pc[j * HEAD_DIM:(j + 1) * HEAD_DIM], qg, HEAD_DIM ** -0.5)
            qT_ref[c0 + j * HEAD_DIM:c0 + (j + 1) * HEAD_DIM, :] = y.astype(BF16)
    pk = jnp.dot(w_ref[qw:qw + kw, :], h, preferred_element_type=F32)
    kn = jnp.concatenate(
        [head_norm(pk[j * HEAD_DIM:(j + 1) * HEAD_DIM], kg, 1.0) for j in range(nkv)], axis=0)
    k_ref[...] = kn.T.astype(BF16)
    pv = jnp.dot(w_ref[qw + kw:qw + 2 * kw, :], h, preferred_element_type=F32)
    vT_ref[...] = pv.astype(BF16)
    g0 = qw + 2 * kw
    for c0 in range(0, gate_w, chunk):
        pg = jnp.dot(w_ref[g0 + c0:g0 + c0 + chunk, :], h, preferred_element_type=F32)
        sgT_ref[c0:c0 + chunk, :] = (pg * (1.0 / (1.0 + jnp.exp(-pg)))).astype(BF16)


def _lane_bcast(v):
    return jnp.broadcast_to(v.astype(F32)[:, None], (v.shape[0], LANES))


def _inproj(xT, norm_gain, w_in, q_gain, k_gain, *, nq, nkv, gate_w, rope_tables=None, tm=512):
    qw, kw = nq * HEAD_DIM, nkv * HEAD_DIM
    in_w = qw + 2 * kw + gate_w
    wT = w_in.T
    rope = rope_tables is not None
    if rope:
        perm = np.concatenate([np.arange(0, HEAD_DIM, 2), np.arange(1, HEAD_DIM, 2)])
        rows = np.arange(in_w)
        nqk = nq + nkv
        rows[:nqk * HEAD_DIM] = (np.arange(nqk)[:, None] * HEAD_DIM + perm[None, :]).reshape(-1)
        wT = wT[rows]
        q_gain, k_gain = q_gain[perm], k_gain[perm]
    wT = wT.astype(BF16)
    const = lambda i: (0, 0)
    col = lambda i: (0, i)
    in_specs = [
        pl.BlockSpec((D_MODEL, tm), col),
        pl.BlockSpec((D_MODEL, LANES), const),
        pl.BlockSpec((in_w, D_MODEL), const),
        pl.BlockSpec((HEAD_DIM, LANES), const),
        pl.BlockSpec((HEAD_DIM, LANES), const),
    ]
    args = [xT, _lane_bcast(norm_gain), wT, _lane_bcast(q_gain), _lane_bcast(k_gain)]
    if rope:
        in_specs += [pl.BlockSpec((HEAD_DIM // 2, tm), col)] * 2
        args += list(rope_tables)
    return pl.pallas_call(
        functools.partial(_inproj_kernel, nq=nq, nkv=nkv, gate_w=gate_w, rope=rope, tm=tm),
        grid=(SEQ // tm,),
        in_specs=in_specs,
        out_specs=[
            pl.BlockSpec((qw, tm), col),
            pl.BlockSpec((tm, kw), lambda i: (i, 0)),
            pl.BlockSpec((kw, tm), col),
            pl.BlockSpec((gate_w, tm), col),
        ],
        out_shape=[
            jax.ShapeDtypeStruct((qw, SEQ), BF16),
            jax.ShapeDtypeStruct((SEQ, kw), BF16),
            jax.ShapeDtypeStruct((kw, SEQ), BF16),
            jax.ShapeDtypeStruct((gate_w, SEQ), BF16),
        ],
        compiler_params=_params("parallel"),
        name="inproj",
    )(*args)


def _fill_qpad(qT_ref, qpad_ref, tq):
    zeros = jnp.zeros((HEAD_DIM, tq), BF16)
    for hh in range(HEADS_PER_PAIR):
        q = qT_ref[hh * HEAD_DIM:(hh + 1) * HEAD_DIM, :]
        lo, hi = (q, zeros) if hh < HEADS_PER_PAIR // 2 else (zeros, q)
        qpad_ref[hh, :HEAD_DIM, :] = lo
        qpad_ref[hh, HEAD_DIM:, :] = hi


def _v_rows(hh):
    kv = hh // (HEADS_PER_PAIR // 2)
    return slice(kv * HEAD_DIM, (kv + 1) * HEAD_DIM)


def _dense_attn_kernel(qT_ref, k_ref, vT_ref, oT_ref, qpad_ref, acc_ref, *, tq, tk):
    _fill_qpad(qT_ref, qpad_ref, tq)
    acc_ref[...] = jnp.zeros(acc_ref.shape, F32)
    m0 = tuple(jnp.full((1, tq), -jnp.inf, F32) for _ in range(HEADS_PER_PAIR))
    l0 = tuple(jnp.zeros((1, tq), F32) for _ in range(HEADS_PER_PAIR))

    def body(j, carry):
        m, l = carry
        off = pl.multiple_of(j * tk, tk)
        kc = k_ref[pl.ds(off, tk), :]
        m_out, l_out = [], []
        for hh in range(HEADS_PER_PAIR):
            vc = vT_ref[_v_rows(hh), pl.ds(off, tk)]
            s = jnp.dot(kc, qpad_ref[hh], preferred_element_type=F32)
            mn = jnp.maximum(m[hh], jnp.max(s, axis=0, keepdims=True))
            alpha = jnp.exp(m[hh] - mn)
            p = jnp.exp(s - mn)
            l_out.append(alpha * l[hh] + jnp.sum(p, axis=0, keepdims=True))
            acc_ref[hh] = alpha * acc_ref[hh] + jnp.dot(
                vc, p.astype(BF16), preferred_element_type=F32)
            m_out.append(mn)
        return tuple(m_out), tuple(l_out)

    _, l = lax.fori_loop(0, SEQ // tk, body, (m0, l0))
    for hh in range(HEADS_PER_PAIR):
        oT_ref[hh * HEAD_DIM:(hh + 1) * HEAD_DIM, :] = (acc_ref[hh] / l[hh]).astype(BF16)


def _dense_attn(qT, k_tm, vT, *, tq=256, tk=512):
    n_pairs = k_tm.shape[1] // LANES
    return pl.pallas_call(
        functools.partial(_dense_attn_kernel, tq=tq, tk=tk),
        grid=(n_pairs, SEQ // tq),
        in_specs=[
            pl.BlockSpec((PAIR_ROWS, tq), lambda p, i: (p, i)),
            pl.BlockSpec((SEQ, LANES), lambda p, i: (0, p)),
            pl.BlockSpec((LANES, SEQ), lambda p, i: (p, 0)),
        ],
        out_specs=pl.BlockSpec((PAIR_ROWS, tq), lambda p, i: (p, i)),
        out_shape=jax.ShapeDtypeStruct(qT.shape, BF16),
        scratch_shapes=[
            pltpu.VMEM((HEADS_PER_PAIR, LANES, tq), BF16),
            pltpu.VMEM((HEADS_PER_PAIR, HEAD_DIM, tq), F32),
        ],
        compiler_params=_params("parallel", "parallel"),
        name="dense_attn",
    )(qT, k_tm, vT)


def _banded_attn_kernel(hp_ref, qT_ref, k_ref, vT_ref, *rest,
                        tq, halo, half_width, dil, pair0, has_sink, want_lse):
    if want_lse:
        oT_ref, lse_ref, qpad_ref = rest
    else:
        oT_ref, qpad_ref = rest
    p_idx = pl.program_id(0)
    i = pl.program_id(1)
    lw = tq + 2 * halo
    wstart = pl.multiple_of(jnp.clip(i * tq - halo, 0, SEQ - lw), LANES)
    _fill_qpad(qT_ref, qpad_ref, tq)
    kpos = wstart + lax.broadcasted_iota(jnp.int32, (lw, tq), 0)
    qpos = i * tq + lax.broadcasted_iota(jnp.int32, (lw, tq), 1)
    rel = kpos - qpos
    arel = jnp.abs(rel)
    neg_dist = jnp.where(arel <= half_width, -arel.astype(F32), -jnp.inf)
    if dil > 1:
        neg_dist = jnp.where((rel & (dil - 1)) == 0, neg_dist, -jnp.inf)
    kw = k_ref[pl.ds(wstart, lw), :]
    head0 = (pair0 + p_idx) * HEADS_PER_PAIR
    for hh in range(HEADS_PER_PAIR):
        slope = hp_ref[0, head0 + hh]
        s = jnp.dot(kw, qpad_ref[hh], preferred_element_type=F32) + slope * neg_dist
        m = jnp.max(s, axis=0, keepdims=True)
        if has_sink:
            sink = hp_ref[1, head0 + hh]
            m = jnp.maximum(m, sink)
        p = jnp.exp(s - m)
        den = jnp.sum(p, axis=0, keepdims=True)
        if has_sink:
            den = den + jnp.exp(sink - m)
        vw = vT_ref[_v_rows(hh), pl.ds(wstart, lw)]
        o = jnp.dot(vw, p.astype(BF16), preferred_element_type=F32) / den
        oT_ref[hh * HEAD_DIM:(hh + 1) * HEAD_DIM, :] = o.astype(BF16)
        if want_lse:
            lse_ref[hh:hh + 1, :] = m + jnp.log(den)


def _banded_attn(head_params, qT, k_tm, vT, *, pair0, n_pairs, half_width, dil, halo,
                 has_sink, want_lse, tq=256):
    out_specs = [pl.BlockSpec((PAIR_ROWS, tq), lambda p, i: (p, i))]
    out_shape = [jax.ShapeDtypeStruct((n_pairs * PAIR_ROWS, SEQ), BF16)]
    if want_lse:
        out_specs.append(pl.BlockSpec((HEADS_PER_PAIR, tq), lambda p, i: (p, i)))
        out_shape.append(jax.ShapeDtypeStruct((n_pairs * HEADS_PER_PAIR, SEQ), F32))
    return pl.pallas_call(
        functools.partial(_banded_attn_kernel, tq=tq, halo=halo, half_width=half_width,
                          dil=dil, pair0=pair0, has_sink=has_sink, want_lse=want_lse),
        grid=(n_pairs, SEQ // tq),
        in_specs=[
            pl.BlockSpec(memory_space=pltpu.SMEM),
            pl.BlockSpec((PAIR_ROWS, tq), lambda p, i: (pair0 + p, i)),
            pl.BlockSpec((SEQ, LANES), lambda p, i: (0, pair0 + p)),
            pl.BlockSpec((LANES, SEQ), lambda p, i: (pair0 + p, 0)),
        ],
        out_specs=out_specs,
        out_shape=out_shape,
        scratch_shapes=[pltpu.VMEM((HEADS_PER_PAIR, LANES, tq), BF16)],
        compiler_params=_params("parallel", "parallel"),
        name="banded_attn",
    )(head_params, qT, k_tm, vT)


def _outproj_kernel(*refs, n_groups, row_major_out):
    if n_groups > 1:
        oT_ref, lse_ref, sgT_ref, w_ref, xT_ref, out_ref = refs
        width = sgT_ref.shape[0]
        slots = width // HEAD_DIM
        lse = [lse_ref[g * slots:(g + 1) * slots, :] for g in range(n_groups)]
        mx = functools.reduce(jnp.maximum, lse)
        e = [jnp.exp(v - mx) for v in lse]
        inv = 1.0 / functools.reduce(lambda a, b: a + b, e)
        merged = []
        for s_ in range(slots):
            rows = slice(s_ * HEAD_DIM, (s_ + 1) * HEAD_DIM)
            acc = None
            for g in range(n_groups):
                wgt = e[g][s_:s_ + 1, :] * inv[s_:s_ + 1, :]
                term = wgt * oT_ref[g * width + s_ * HEAD_DIM:g * width + (s_ + 1) * HEAD_DIM, :].astype(F32)
                acc = term if acc is None else acc + term
            merged.append(acc.astype(BF16) * sgT_ref[rows, :])
        og = jnp.concatenate(merged, axis=0)
    else:
        oT_ref, sgT_ref, w_ref, xT_ref, out_ref = refs
        og = oT_ref[...] * sgT_ref[...]
    y = xT_ref[...] + jnp.dot(w_ref[...], og, preferred_element_type=F32)
    out_ref[...] = y.T if row_major_out else y


def _outproj(oT, lse, sgT, w_out, xT, *, row_major_out=False, tm=512):
    width = sgT.shape[0]
    n_groups = oT.shape[0] // width
    woT = w_out.T.astype(BF16)
    col = lambda i: (0, i)
    in_specs = [pl.BlockSpec((oT.shape[0], tm), col)]
    args = [oT]
    if n_groups > 1:
        in_specs.append(pl.BlockSpec((lse.shape[0], tm), col))
        args.append(lse)
    in_specs += [
        pl.BlockSpec((width, tm), col),
        pl.BlockSpec((D_MODEL, width), lambda i: (0, 0)),
        pl.BlockSpec((D_MODEL, tm), col),
    ]
    args += [sgT, woT, xT]
    if row_major_out:
        out_spec = pl.BlockSpec((tm, D_MODEL), lambda i: (i, 0))
        out_shape = jax.ShapeDtypeStruct((SEQ, D_MODEL), F32)
    else:
        out_spec = pl.BlockSpec((D_MODEL, tm), col)
        out_shape = jax.ShapeDtypeStruct((D_MODEL, SEQ), F32)
    return pl.pallas_call(
        functools.partial(_outproj_kernel, n_groups=n_groups, row_major_out=row_major_out),
        grid=(SEQ // tm,),
        in_specs=in_specs,
        out_specs=out_spec,
        out_shape=out_shape,
        compiler_params=_params("parallel"),
        name="outproj",
    )(*args)


def _alibi_slopes(n):
    return jnp.asarray(2.0 ** (-8.0 * np.arange(1, n + 1) / n), dtype=F32)


def _rope_tables():
    t = np.arange(SEQ)
    axis_dim = HEAD_DIM // 2
    freqs = 1.0 / (ROPE_THETA ** (jnp.arange(0, axis_dim, 2, dtype=F32) / axis_dim))
    row = jnp.asarray(t // GRID_W, F32)
    col = jnp.asarray(t % GRID_W, F32)
    ang = jnp.concatenate([freqs[:, None] * row[None, :], freqs[:, None] * col[None, :]], axis=0)
    return jnp.cos(ang), jnp.sin(ang)


def _mixer_a(xT, norm, w_in, q_gain, k_gain, sink, w_out, *, last):
    qT, k_tm, vT, sgT = _inproj(xT, norm, w_in, q_gain, k_gain, nq=16, nkv=4, gate_w=1024)
    hp = jnp.stack([_alibi_slopes(16), sink.astype(F32)])
    (oT,) = _banded_attn(hp, qT, k_tm, vT, pair0=0, n_pairs=2, half_width=A_WINDOW, dil=1,
                         halo=A_WINDOW, has_sink=True, want_lse=False)
    return _outproj(oT, None, sgT, w_out, xT, row_major_out=last)


def _mixer_b(xT, norm, w_in, q_gain, k_gain, w_out):
    qT, k_tm, vT, sgT = _inproj(xT, norm, w_in, q_gain, k_gain, nq=16, nkv=4, gate_w=1024,
                                rope_tables=_rope_tables())
    oT = _dense_attn(qT, k_tm, vT)
    return _outproj(oT, None, sgT, w_out, xT)


def _mixer_c(xT, norm, w_in, q_gain, k_gain, w_out):
    qT, k_tm, vT, sgT = _inproj(xT, norm, w_in, q_gain, k_gain, nq=24, nkv=6, gate_w=512)
    slopes = _alibi_slopes(24)
    hp = jnp.stack([slopes, jnp.zeros_like(slopes)])
    outs, lses = [], []
    for gi, (window, dil) in enumerate(C_GROUPS):
        half_width = window // 2
        halo = -(-half_width // LANES) * LANES
        o, lse = _banded_attn(hp, qT, k_tm, vT, pair0=gi, n_pairs=1, half_width=half_width,
                              dil=dil, halo=halo, has_sink=False, want_lse=True)
        outs.append(o)
        lses.append(lse)
    return _outproj(jnp.concatenate(outs, axis=0), jnp.concatenate(lses, axis=0), sgT, w_out, xT)


def kernel(x, l0_norm, l0_w_in, l0_q_gain, l0_k_gain, l0_sink, l0_w_out,
           l1_norm, l1_w_in, l1_q_gain, l1_k_gain, l1_w_out,
           l2_norm, l2_w_in, l2_q_gain, l2_k_gain, l2_w_out,
           l3_norm, l3_w_in, l3_q_gain, l3_k_gain, l3_sink, l3_w_out):
    xT = _to_feature_major(x.reshape(SEQ, D_MODEL))
    xT = _mixer_a(xT, l0_norm, l0_w_in, l0_q_gain, l0_k_gain, l0_sink, l0_w_out, last=False)
    xT = _mixer_b(xT, l1_norm, l1_w_in, l1_q_gain, l1_k_gain, l1_w_out)
    xT = _mixer_c(xT, l2_norm, l2_w_in, l2_q_gain, l2_k_gain, l2_w_out)
    out = _mixer_a(xT, l3_norm, l3_w_in, l3_q_gain, l3_k_gain, l3_sink, l3_w_out, last=True)
    return out.reshape(x.shape)
```

```python
import functools

import numpy as np
import jax
import jax.numpy as jnp
from jax import lax
from jax.experimental import pallas as pl
from jax.experimental.pallas import tpu as pltpu

D_MODEL = 1024
SEQ = 16384
HEAD_DIM = 64
NORM_EPS = 1e-6
GRID_W = 64
ROPE_THETA = 10000.0
A_WINDOW = 128
C_GROUPS = ((128, 1), (512, 4), (2048, 16))

LANES = 128
HEADS_PER_PAIR = 8
PAIR_ROWS = HEADS_PER_PAIR * HEAD_DIM
VMEM_LIMIT = 56 * 1024 * 1024

BF16 = jnp.bfloat16
F32 = jnp.float32


def _params(*sem):
    return pltpu.CompilerParams(dimension_semantics=sem, vmem_limit_bytes=VMEM_LIMIT)


def _transpose_kernel(x_ref, o_ref):
    o_ref[...] = x_ref[...].T


def _to_feature_major(x2d, tm=512):
    return pl.pallas_call(
        _transpose_kernel,
        grid=(SEQ // tm,),
        in_specs=[pl.BlockSpec((tm, D_MODEL), lambda i: (i, 0))],
        out_specs=pl.BlockSpec((D_MODEL, tm), lambda i: (0, i)),
        out_shape=jax.ShapeDtypeStruct((D_MODEL, SEQ), F32),
        compiler_params=_params("parallel"),
        name="to_feature_major",
    )(x2d)


def _inproj_kernel(*refs, nq, nkv, gate_w, rope, q_scale, tm):
    if rope:
        (xT_ref, ng_ref, w_ref, qg_ref, kg_ref, cos_ref, sin_ref,
         qT_ref, k_ref, vT_ref, sgT_ref) = refs
        cos, sin = cos_ref[...], sin_ref[...]
    else:
        xT_ref, ng_ref, w_ref, qg_ref, kg_ref, qT_ref, k_ref, vT_ref, sgT_ref = refs
    rep = tm // LANES
    x = xT_ref[...]
    r = lax.rsqrt(jnp.mean(x * x, axis=0, keepdims=True) + NORM_EPS)
    h = (x * r * pltpu.repeat(ng_ref[...], rep, axis=1)).astype(BF16)
    qg = pltpu.repeat(qg_ref[...], rep, axis=1)
    kg = pltpu.repeat(kg_ref[...], rep, axis=1)

    def head_norm(ph, gain, scale):
        ss = jnp.sum(ph * ph, axis=0, keepdims=True)
        y = ph * (lax.rsqrt(ss * (1.0 / HEAD_DIM) + NORM_EPS) * scale) * gain
        if rope:
            half = HEAD_DIM // 2
            x1, x2 = y[:half], y[half:]
            y = jnp.concatenate([x1 * cos - x2 * sin, x1 * sin + x2 * cos], axis=0)
        return y

    qw, kw = nq * HEAD_DIM, nkv * HEAD_DIM
    chunk = 256
    for c0 in range(0, qw, chunk):
        pc = jnp.dot(w_ref[c0:c0 + chunk, :], h, preferred_element_type=F32)
        for j in range(chunk // HEAD_DIM):
            y = head_norm(pc[j * HEAD_DIM:(j + 1) * HEAD_DIM], qg, q_scale)
            qT_ref[c0 + j * HEAD_DIM:c0 + (j + 1) * HEAD_DIM, :] = y.astype(BF16)
    pk = jnp.dot(w_ref[qw:qw + kw, :], h, preferred_element_type=F32)
    kn = jnp.concatenate(
        [head_norm(pk[j * HEAD_DIM:(j + 1) * HEAD_DIM], kg, 1.0) for j in range(nkv)], axis=0)
    k_ref[...] = kn.T.astype(BF16)
    pv = jnp.dot(w_ref[qw + kw:qw + 2 * kw, :], h, preferred_element_type=F32)
    vT_ref[...] = pv.astype(BF16)
    g0 = qw + 2 * kw
    for c0 in range(0, gate_w, chunk):
        pg = jnp.dot(w_ref[g0 + c0:g0 + c0 + chunk, :], h, preferred_element_type=F32)
        sgT_ref[c0:c0 + chunk, :] = (pg * (1.0 / (1.0 + jnp.exp(-pg)))).astype(BF16)


def _lane_bcast(v):
    return jnp.broadcast_to(v.astype(F32)[:, None], (v.shape[0], LANES))


def _inproj(xT, norm_gain, w_in, q_gain, k_gain, *, nq, nkv, gate_w, rope_tables=None,
            q_scale=HEAD_DIM ** -0.5, tm=512):
    qw, kw = nq * HEAD_DIM, nkv * HEAD_DIM
    in_w = qw + 2 * kw + gate_w
    wT = w_in.T
    rope = rope_tables is not None
    if rope:
        perm = np.concatenate([np.arange(0, HEAD_DIM, 2), np.arange(1, HEAD_DIM, 2)])
        rows = np.arange(in_w)
        nqk = nq + nkv
        rows[:nqk * HEAD_DIM] = (np.arange(nqk)[:, None] * HEAD_DIM + perm[None, :]).reshape(-1)
        wT = wT[rows]
        q_gain, k_gain = q_gain[perm], k_gain[perm]
    wT = wT.astype(BF16)
    const = lambda i: (0, 0)
    col = lambda i: (0, i)
    in_specs = [
        pl.BlockSpec((D_MODEL, tm), col),
        pl.BlockSpec((D_MODEL, LANES), const),
        pl.BlockSpec((in_w, D_MODEL), const),
        pl.BlockSpec((HEAD_DIM, LANES), const),
        pl.BlockSpec((HEAD_DIM, LANES), const),
    ]
    args = [xT, _lane_bcast(norm_gain), wT, _lane_bcast(q_gain), _lane_bcast(k_gain)]
    if rope:
        in_specs += [pl.BlockSpec((HEAD_DIM // 2, tm), col)] * 2
        args += list(rope_tables)
    return pl.pallas_call(
        functools.partial(_inproj_kernel, nq=nq, nkv=nkv, gate_w=gate_w, rope=rope,
                          q_scale=q_scale, tm=tm),
        grid=(SEQ // tm,),
        in_specs=in_specs,
        out_specs=[
            pl.BlockSpec((qw, tm), col),
            pl.BlockSpec((tm, kw), lambda i: (i, 0)),
            pl.BlockSpec((kw, tm), col),
            pl.BlockSpec((gate_w, tm), col),
        ],
        out_shape=[
            jax.ShapeDtypeStruct((qw, SEQ), BF16),
            jax.ShapeDtypeStruct((SEQ, kw), BF16),
            jax.ShapeDtypeStruct((kw, SEQ), BF16),
            jax.ShapeDtypeStruct((gate_w, SEQ), BF16),
        ],
        compiler_params=_params("parallel"),
        name="inproj",
    )(*args)


def _fill_qpad(qT_ref, qpad_ref, tq):
    zeros = jnp.zeros((HEAD_DIM, tq), BF16)
    for hh in range(HEADS_PER_PAIR):
        q = qT_ref[hh * HEAD_DIM:(hh + 1) * HEAD_DIM, :]
        lo, hi = (q, zeros) if hh < HEADS_PER_PAIR // 2 else (zeros, q)
        qpad_ref[hh, :HEAD_DIM, :] = lo
        qpad_ref[hh, HEAD_DIM:, :] = hi


def _v_rows(hh):
    kv = hh // (HEADS_PER_PAIR // 2)
    return slice(kv * HEAD_DIM, (kv + 1) * HEAD_DIM)


def _dense_attn_kernel(qT_ref, k_ref, vT_ref, oT_ref, qpad_ref, acc_ref, s0_ref, s1_ref,
                       *, tq, tk):
    n_chunks = SEQ // tk
    _fill_qpad(qT_ref, qpad_ref, tq)
    acc_ref[...] = jnp.zeros(acc_ref.shape, F32)
    m0 = tuple(jnp.full((1, tq), -jnp.inf, F32) for _ in range(HEADS_PER_PAIR))
    l0 = tuple(jnp.zeros((1, tq), F32) for _ in range(HEADS_PER_PAIR))

    def scores(c, s_ref, hh):
        off = pl.multiple_of(c * tk, tk)
        s = jnp.dot(k_ref[pl.ds(off, tk), :], qpad_ref[hh], preferred_element_type=F32)
        s_ref[hh] = s
        return jnp.max(s, axis=0, keepdims=True)

    def accumulate(c, s_ref, hh, mc, m, l):
        off = pl.multiple_of(c * tk, tk)
        mn = jnp.maximum(m, mc)
        alpha = jnp.exp2(m - mn)
        p = jnp.exp2(s_ref[hh] - mn)
        l = alpha * l + jnp.sum(p, axis=0, keepdims=True)
        acc_ref[hh] = alpha * acc_ref[hh] + jnp.dot(
            vT_ref[_v_rows(hh), pl.ds(off, tk)], p.astype(BF16), preferred_element_type=F32)
        return mn, l

    def stage(c_acc, s_acc, mc, m, l, c_next, s_next):
        m_out, l_out, mc_out = [], [], []
        for hh in range(HEADS_PER_PAIR):
            if c_next is not None:
                mc_out.append(scores(c_next, s_next, hh))
            mn, ln = accumulate(c_acc, s_acc, hh, mc[hh], m[hh], l[hh])
            m_out.append(mn)
            l_out.append(ln)
        return tuple(mc_out), tuple(m_out), tuple(l_out)

    mc = tuple(scores(0, s0_ref, hh) for hh in range(HEADS_PER_PAIR))

    def body(jj, carry):
        mc, m, l = carry
        c = 2 * jj
        mc, m, l = stage(c, s0_ref, mc, m, l, c + 1, s1_ref)
        mc, m, l = stage(c + 1, s1_ref, mc, m, l, c + 2, s0_ref)
        return mc, m, l

    mc, m, l = lax.fori_loop(0, n_chunks // 2 - 1, body, (mc, m0, l0))
    mc, m, l = stage(n_chunks - 2, s0_ref, mc, m, l, n_chunks - 1, s1_ref)
    _, m, l = stage(n_chunks - 1, s1_ref, mc, m, l, None, None)
    for hh in range(HEADS_PER_PAIR):
        oT_ref[hh * HEAD_DIM:(hh + 1) * HEAD_DIM, :] = (acc_ref[hh] / l[hh]).astype(BF16)


def _dense_attn(qT, k_tm, vT, *, tq=256, tk=512):
    n_pairs = k_tm.shape[1] // LANES
    s_scratch = pltpu.VMEM((HEADS_PER_PAIR, tk, tq), F32)
    return pl.pallas_call(
        functools.partial(_dense_attn_kernel, tq=tq, tk=tk),
        grid=(n_pairs, SEQ // tq),
        in_specs=[
            pl.BlockSpec((PAIR_ROWS, tq), lambda p, i: (p, i)),
            pl.BlockSpec((SEQ, LANES), lambda p, i: (0, p)),
            pl.BlockSpec((LANES, SEQ), lambda p, i: (p, 0)),
        ],
        out_specs=pl.BlockSpec((PAIR_ROWS, tq), lambda p, i: (p, i)),
        out_shape=jax.ShapeDtypeStruct(qT.shape, BF16),
        scratch_shapes=[
            pltpu.VMEM((HEADS_PER_PAIR, LANES, tq), BF16),
            pltpu.VMEM((HEADS_PER_PAIR, HEAD_DIM, tq), F32),
            s_scratch,
            s_scratch,
        ],
        compiler_params=_params("parallel", "parallel"),
        name="dense_attn",
    )(qT, k_tm, vT)


def _banded_attn_kernel(hp_ref, qT_ref, k_ref, vT_ref, *rest,
                        tq, halo, half_width, dil, pair0, has_sink, want_lse):
    if want_lse:
        oT_ref, lse_ref, qpad_ref = rest
    else:
        oT_ref, qpad_ref = rest
    p_idx = pl.program_id(0)
    i = pl.program_id(1)
    lw = tq + 2 * halo
    wstart = pl.multiple_of(jnp.clip(i * tq - halo, 0, SEQ - lw), LANES)
    _fill_qpad(qT_ref, qpad_ref, tq)
    kpos = wstart + lax.broadcasted_iota(jnp.int32, (lw, tq), 0)
    qpos = i * tq + lax.broadcasted_iota(jnp.int32, (lw, tq), 1)
    rel = kpos - qpos
    arel = jnp.abs(rel)
    neg_dist = jnp.where(arel <= half_width, -arel.astype(F32), -jnp.inf)
    if dil > 1:
        neg_dist = jnp.where((rel & (dil - 1)) == 0, neg_dist, -jnp.inf)
    kw = k_ref[pl.ds(wstart, lw), :]
    head0 = (pair0 + p_idx) * HEADS_PER_PAIR
    for hh in range(HEADS_PER_PAIR):
        slope = hp_ref[0, head0 + hh]
        s = jnp.dot(kw, qpad_ref[hh], preferred_element_type=F32) + slope * neg_dist
        m = jnp.max(s, axis=0, keepdims=True)
        if has_sink:
            sink = hp_ref[1, head0 + hh]
            m = jnp.maximum(m, sink)
        p = jnp.exp(s - m)
        den = jnp.sum(p, axis=0, keepdims=True)
        if has_sink:
            den = den + jnp.exp(sink - m)
        vw = vT_ref[_v_rows(hh), pl.ds(wstart, lw)]
        o = jnp.dot(vw, p.astype(BF16), preferred_element_type=F32) / den
        oT_ref[hh * HEAD_DIM:(hh + 1) * HEAD_DIM, :] = o.astype(BF16)
        if want_lse:
            lse_ref[hh:hh + 1, :] = m + jnp.log(den)


def _banded_attn(head_params, qT, k_tm, vT, *, pair0, n_pairs, half_width, dil, halo,
                 has_sink, want_lse, tq=256):
    out_specs = [pl.BlockSpec((PAIR_ROWS, tq), lambda p, i: (p, i))]
    out_shape = [jax.ShapeDtypeStruct((n_pairs * PAIR_ROWS, SEQ), BF16)]
    if want_lse:
        out_specs.append(pl.BlockSpec((HEADS_PER_PAIR, tq), lambda p, i: (p, i)))
        out_shape.append(jax.ShapeDtypeStruct((n_pairs * HEADS_PER_PAIR, SEQ), F32))
    return pl.pallas_call(
        functools.partial(_banded_attn_kernel, tq=tq, halo=halo, half_width=half_width,
                          dil=dil, pair0=pair0, has_sink=has_sink, want_lse=want_lse),
        grid=(n_pairs, SEQ // tq),
        in_specs=[
            pl.BlockSpec(memory_space=pltpu.SMEM),
            pl.BlockSpec((PAIR_ROWS, tq), lambda p, i: (pair0 + p, i)),
            pl.BlockSpec((SEQ, LANES), lambda p, i: (0, pair0 + p)),
            pl.BlockSpec((LANES, SEQ), lambda p, i: (pair0 + p, 0)),
        ],
        out_specs=out_specs,
        out_shape=out_shape,
        scratch_shapes=[pltpu.VMEM((HEADS_PER_PAIR, LANES, tq), BF16)],
        compiler_params=_params("parallel", "parallel"),
        name="banded_attn",
    )(head_params, qT, k_tm, vT)


def _outproj_kernel(*refs, n_groups, row_major_out):
    if n_groups > 1:
        oT_ref, lse_ref, sgT_ref, w_ref, xT_ref, out_ref = refs
        width = sgT_ref.shape[0]
        slots = width // HEAD_DIM
        lse = [lse_ref[g * slots:(g + 1) * slots, :] for g in range(n_groups)]
        mx = functools.reduce(jnp.maximum, lse)
        e = [jnp.exp(v - mx) for v in lse]
        inv = 1.0 / functools.reduce(lambda a, b: a + b, e)
        merged = []
        for s_ in range(slots):
            rows = slice(s_ * HEAD_DIM, (s_ + 1) * HEAD_DIM)
            acc = None
            for g in range(n_groups):
                wgt = e[g][s_:s_ + 1, :] * inv[s_:s_ + 1, :]
                term = wgt * oT_ref[g * width + s_ * HEAD_DIM:g * width + (s_ + 1) * HEAD_DIM, :].astype(F32)
                acc = term if acc is None else acc + term
            merged.append(acc.astype(BF16) * sgT_ref[rows, :])
        og = jnp.concatenate(merged, axis=0)
    else:
        oT_ref, sgT_ref, w_ref, xT_ref, out_ref = refs
        og = oT_ref[...] * sgT_ref[...]
    y = xT_ref[...] + jnp.dot(w_ref[...], og, preferred_element_type=F32)
    out_ref[...] = y.T if row_major_out else y


def _outproj(oT, lse, sgT, w_out, xT, *, row_major_out=False, tm=512):
    width = sgT.shape[0]
    n_groups = oT.shape[0] // width
    woT = w_out.T.astype(BF16)
    col = lambda i: (0, i)
    in_specs = [pl.BlockSpec((oT.shape[0], tm), col)]
    args = [oT]
    if n_groups > 1:
        in_specs.append(pl.BlockSpec((lse.shape[0], tm), col))
        args.append(lse)
    in_specs += [
        pl.BlockSpec((width, tm), col),
        pl.BlockSpec((D_MODEL, width), lambda i: (0, 0)),
        pl.BlockSpec((D_MODEL, tm), col),
    ]
    args += [sgT, woT, xT]
    if row_major_out:
        out_spec = pl.BlockSpec((tm, D_MODEL), lambda i: (i, 0))
        out_shape = jax.ShapeDtypeStruct((SEQ, D_MODEL), F32)
    else:
        out_spec = pl.BlockSpec((D_MODEL, tm), col)
        out_shape = jax.ShapeDtypeStruct((D_MODEL, SEQ), F32)
    return pl.pallas_call(
        functools.partial(_outproj_kernel, n_groups=n_groups, row_major_out=row_major_out),
        grid=(SEQ // tm,),
        in_specs=in_specs,
        out_specs=out_spec,
        out_shape=out_shape,
        compiler_params=_params("parallel"),
        name="outproj",
    )(*args)


def _alibi_slopes(n):
    return jnp.asarray(2.0 ** (-8.0 * np.arange(1, n + 1) / n), dtype=F32)


def _rope_tables():
    t = np.arange(SEQ)
    axis_dim = HEAD_DIM // 2
    freqs = 1.0 / (ROPE_THETA ** (jnp.arange(0, axis_dim, 2, dtype=F32) / axis_dim))
    row = jnp.asarray(t // GRID_W, F32)
    col = jnp.asarray(t % GRID_W, F32)
    ang = jnp.concatenate([freqs[:, None] * row[None, :], freqs[:, None] * col[None, :]], axis=0)
    return jnp.cos(ang), jnp.sin(ang)


def _mixer_a(xT, norm, w_in, q_gain, k_gain, sink, w_out, *, last):
    qT, k_tm, vT, sgT = _inproj(xT, norm, w_in, q_gain, k_gain, nq=16, nkv=4, gate_w=1024)
    hp = jnp.stack([_alibi_slopes(16), sink.astype(F32)])
    (oT,) = _banded_attn(hp, qT, k_tm, vT, pair0=0, n_pairs=2, half_width=A_WINDOW, dil=1,
                         halo=A_WINDOW, has_sink=True, want_lse=False)
    return _outproj(oT, None, sgT, w_out, xT, row_major_out=last)


def _mixer_b(xT, norm, w_in, q_gain, k_gain, w_out):
    qT, k_tm, vT, sgT = _inproj(xT, norm, w_in, q_gain, k_gain, nq=16, nkv=4, gate_w=1024,
                                rope_tables=_rope_tables(),
                                q_scale=HEAD_DIM ** -0.5 * float(np.log2(np.e)))
    oT = _dense_attn(qT, k_tm, vT)
    return _outproj(oT, None, sgT, w_out, xT)


def _mixer_c(xT, norm, w_in, q_gain, k_gain, w_out):
    qT, k_tm, vT, sgT = _inproj(xT, norm, w_in, q_gain, k_gain, nq=24, nkv=6, gate_w=512)
    slopes = _alibi_slopes(24)
    hp = jnp.stack([slopes, jnp.zeros_like(slopes)])
    outs, lses = [], []
    for gi, (window, dil) in enumerate(C_GROUPS):
        half_width = window // 2
        halo = -(-half_width // LANES) * LANES
        o, lse = _banded_attn(hp, qT, k_tm, vT, pair0=gi, n_pairs=1, half_width=half_width,
                              dil=dil, halo=halo, has_sink=False, want_lse=True)
        outs.append(o)
        lses.append(lse)
    return _outproj(jnp.concatenate(outs, axis=0), jnp.concatenate(lses, axis=0), sgT, w_out, xT)


def kernel(x, l0_norm, l0_w_in, l0_q_gain, l0_k_gain, l0_sink, l0_w_out,
           l1_norm, l1_w_in, l1_q_gain, l1_k_gain, l1_w_out,
           l2_norm, l2_w_in, l2_q_gain, l2_k_gain, l2_w_out,
           l3_norm, l3_w_in, l3_q_gain, l3_k_gain, l3_sink, l3_w_out):
    xT = _to_feature_major(x.reshape(SEQ, D_MODEL))
    xT = _mixer_a(xT, l0_norm, l0_w_in, l0_q_gain, l0_k_gain, l0_sink, l0_w_out, last=False)
    xT = _mixer_b(xT, l1_norm, l1_w_in, l1_q_gain, l1_k_gain, l1_w_out)
    xT = _mixer_c(xT, l2_norm, l2_w_in, l2_q_gain, l2_k_gain, l2_w_out)
    out = _mixer_a(xT, l3_norm, l3_w_in, l3_q_gain, l3_k_gain, l3_sink, l3_w_out, last=True)
    return out.reshape(x.shape)
```

```python
import functools
from typing import NamedTuple

import numpy as np
import jax
import jax.numpy as jnp
from jax import lax
from jax.experimental import pallas as pl
from jax.experimental.pallas import tpu as pltpu

D_MODEL = 1024
SEQ = 16384
HEAD_DIM = 64
NORM_EPS = 1e-6
GRID_W = 64
ROPE_THETA = 10000.0
A_WINDOW = 128
C_GROUPS = ((128, 1), (512, 4), (2048, 16))
LOG2E = float(np.log2(np.e))
Q_SCALE = HEAD_DIM ** -0.5 * LOG2E

LANES = 128
BF16_SUBLANES = 16
Q_PER_KV = 4
HEADS_PER_PAIR = 2 * Q_PER_KV
PAIR_ROWS = HEADS_PER_PAIR * HEAD_DIM
V_AUG_ROWS = HEAD_DIM + BF16_SUBLANES
VMEM_LIMIT = 56 * 1024 * 1024

BF16 = jnp.bfloat16
F32 = jnp.float32


def _params(*sem):
    return pltpu.CompilerParams(dimension_semantics=sem, vmem_limit_bytes=VMEM_LIMIT)


def _tile_lanes(x, rep):
    return x if rep == 1 else jnp.concatenate([x] * rep, axis=1)


def _resident(shape):
    return pl.BlockSpec(shape, lambda *_: (0,) * len(shape), pipeline_mode=pl.Buffered(1))


def _transpose_kernel(x_ref, o_ref):
    o_ref[...] = x_ref[...].T


def _to_feature_major(x2d, tm=512):
    return pl.pallas_call(
        _transpose_kernel,
        grid=(SEQ // tm,),
        in_specs=[pl.BlockSpec((tm, D_MODEL), lambda i: (i, 0))],
        out_specs=pl.BlockSpec((D_MODEL, tm), lambda i: (0, i)),
        out_shape=jax.ShapeDtypeStruct((D_MODEL, SEQ), F32),
        compiler_params=_params("parallel"),
        name="to_feature_major",
    )(x2d)


def _inproj_kernel(*refs, nq, nkv, gate_w, rope, tm):
    if rope:
        (xT_ref, ng_ref, w_ref, qg_ref, kg_ref, cos_ref, sin_ref,
         qT_ref, k_ref, vT_ref, sgT_ref) = refs
        cos, sin = cos_ref[...], sin_ref[...]
    else:
        xT_ref, ng_ref, w_ref, qg_ref, kg_ref, qT_ref, k_ref, vT_ref, sgT_ref = refs
    rep = tm // LANES
    x = xT_ref[...]
    r = lax.rsqrt(jnp.mean(x * x, axis=0, keepdims=True) + NORM_EPS)
    h = (x * r * _tile_lanes(ng_ref[...], rep)).astype(BF16)
    qg = _tile_lanes(qg_ref[...], rep)
    kg = _tile_lanes(kg_ref[...], rep)

    def head_norm(ph, gain, scale):
        ss = jnp.sum(ph * ph, axis=0, keepdims=True)
        y = ph * (lax.rsqrt(ss * (1.0 / HEAD_DIM) + NORM_EPS) * scale) * gain
        if rope:
            half = HEAD_DIM // 2
            x1, x2 = y[:half], y[half:]
            y = jnp.concatenate([x1 * cos - x2 * sin, x1 * sin + x2 * cos], axis=0)
        return y

    qw, kw = nq * HEAD_DIM, nkv * HEAD_DIM
    chunk = 256
    for c0 in range(0, qw, chunk):
        pc = jnp.dot(w_ref[c0:c0 + chunk, :], h, preferred_element_type=F32)
        for j in range(chunk // HEAD_DIM):
            y = head_norm(pc[j * HEAD_DIM:(j + 1) * HEAD_DIM], qg, Q_SCALE)
            qT_ref[c0 + j * HEAD_DIM:c0 + (j + 1) * HEAD_DIM, :] = y.astype(BF16)
    pk = jnp.dot(w_ref[qw:qw + kw, :], h, preferred_element_type=F32)
    kn = jnp.concatenate(
        [head_norm(pk[j * HEAD_DIM:(j + 1) * HEAD_DIM], kg, 1.0) for j in range(nkv)], axis=0)
    k_ref[...] = kn.T.astype(BF16)
    pv = jnp.dot(w_ref[qw + kw:qw + 2 * kw, :], h, preferred_element_type=F32)
    vT_ref[...] = pv.astype(BF16)
    g0 = qw + 2 * kw
    for c0 in range(0, gate_w, chunk):
        pg = jnp.dot(w_ref[g0 + c0:g0 + c0 + chunk, :], h, preferred_element_type=F32)
        sgT_ref[c0:c0 + chunk, :] = (pg * (1.0 / (1.0 + jnp.exp(-pg)))).astype(BF16)


def _lane_bcast(v):
    return jnp.broadcast_to(v.astype(F32)[:, None], (v.shape[0], LANES))


def _inproj(xT, norm_gain, w_in, q_gain, k_gain, *, nq, nkv, gate_w, rope_tables=None, tm=512):
    qw, kw = nq * HEAD_DIM, nkv * HEAD_DIM
    in_w = qw + 2 * kw + gate_w
    wT = w_in.T
    rope = rope_tables is not None
    if rope:
        perm = np.concatenate([np.arange(0, HEAD_DIM, 2), np.arange(1, HEAD_DIM, 2)])
        rows = np.arange(in_w)
        nqk = nq + nkv
        rows[:nqk * HEAD_DIM] = (np.arange(nqk)[:, None] * HEAD_DIM + perm[None, :]).reshape(-1)
        wT = wT[rows]
        q_gain, k_gain = q_gain[perm], k_gain[perm]
    wT = wT.astype(BF16)
    const = lambda i: (0, 0)
    col = lambda i: (0, i)
    in_specs = [
        pl.BlockSpec((D_MODEL, tm), col),
        pl.BlockSpec((D_MODEL, LANES), const),
        pl.BlockSpec((in_w, D_MODEL), const),
        pl.BlockSpec((HEAD_DIM, LANES), const),
        pl.BlockSpec((HEAD_DIM, LANES), const),
    ]
    args = [xT, _lane_bcast(norm_gain), wT, _lane_bcast(q_gain), _lane_bcast(k_gain)]
    if rope:
        in_specs += [pl.BlockSpec((HEAD_DIM // 2, tm), col)] * 2
        args += list(rope_tables)
    return pl.pallas_call(
        functools.partial(_inproj_kernel, nq=nq, nkv=nkv, gate_w=gate_w, rope=rope, tm=tm),
        grid=(SEQ // tm,),
        in_specs=in_specs,
        out_specs=[
            pl.BlockSpec((qw, tm), col),
            pl.BlockSpec((tm, kw), lambda i: (i, 0)),
            pl.BlockSpec((kw, tm), col),
            pl.BlockSpec((gate_w, tm), col),
        ],
        out_shape=[
            jax.ShapeDtypeStruct((qw, SEQ), BF16),
            jax.ShapeDtypeStruct((SEQ, kw), BF16),
            jax.ShapeDtypeStruct((kw, SEQ), BF16),
            jax.ShapeDtypeStruct((gate_w, SEQ), BF16),
        ],
        compiler_params=_params("parallel"),
        name="inproj",
    )(*args)


def _fill_qpad(qT_ref, qpad_ref, head0, n_heads, tq):
    zeros = jnp.zeros((HEAD_DIM, tq), BF16)
    for h in range(n_heads):
        q = qT_ref[h * HEAD_DIM:(h + 1) * HEAD_DIM, :]
        lo, hi = (q, zeros) if ((head0 + h) // Q_PER_KV) % 2 == 0 else (zeros, q)
        qpad_ref[h, :HEAD_DIM, :] = lo
        qpad_ref[h, HEAD_DIM:, :] = hi


def _ones_rows(n):
    row = lax.broadcasted_iota(jnp.int32, (BF16_SUBLANES, n), 0)
    return jnp.where(row == 0, 1.0, 0.0).astype(BF16)


def _v_aug(vT_ref, kv, cols, ones_rows):
    return jnp.concatenate([vT_ref[kv * HEAD_DIM:(kv + 1) * HEAD_DIM, cols], ones_rows], axis=0)


def _dense_attn_kernel(qT_ref, k_ref, vT_ref, oT_ref, qpad_ref, acc_ref, s0_ref, s1_ref,
                       *, tq, tk):
    n_chunks = SEQ // tk
    _fill_qpad(qT_ref, qpad_ref, 0, HEADS_PER_PAIR, tq)
    acc_ref[...] = jnp.zeros(acc_ref.shape, F32)
    ones_rows = _ones_rows(tk)
    m0 = tuple(jnp.full((1, tq), -jnp.inf, F32) for _ in range(HEADS_PER_PAIR))

    def scores(c, s_ref, hh):
        off = pl.multiple_of(c * tk, tk)
        s = jnp.dot(k_ref[pl.ds(off, tk), :], qpad_ref[hh], preferred_element_type=F32)
        s_ref[hh] = s
        return jnp.max(s, axis=0, keepdims=True)

    def accumulate(c, s_ref, hh, mc, m):
        off = pl.multiple_of(c * tk, tk)
        mn = jnp.maximum(m, mc)
        p = jnp.exp2(s_ref[hh] - mn).astype(BF16)
        acc_ref[hh] = jnp.exp2(m - mn) * acc_ref[hh] + jnp.dot(
            _v_aug(vT_ref, hh // Q_PER_KV, pl.ds(off, tk), ones_rows), p,
            preferred_element_type=F32)
        return mn

    def stage(c_acc, s_acc, mc, m, c_next, s_next):
        m_out, mc_out = [], []
        for hh in range(HEADS_PER_PAIR):
            if c_next is not None:
                mc_out.append(scores(c_next, s_next, hh))
            m_out.append(accumulate(c_acc, s_acc, hh, mc[hh], m[hh]))
        return tuple(mc_out), tuple(m_out)

    mc = tuple(scores(0, s0_ref, hh) for hh in range(HEADS_PER_PAIR))

    def body(jj, carry):
        mc, m = carry
        c = 2 * jj
        mc, m = stage(c, s0_ref, mc, m, c + 1, s1_ref)
        mc, m = stage(c + 1, s1_ref, mc, m, c + 2, s0_ref)
        return mc, m

    mc, m = lax.fori_loop(0, n_chunks // 2 - 1, body, (mc, m0))
    mc, m = stage(n_chunks - 2, s0_ref, mc, m, n_chunks - 1, s1_ref)
    stage(n_chunks - 1, s1_ref, mc, m, None, None)
    for hh in range(HEADS_PER_PAIR):
        inv = 1.0 / acc_ref[hh, HEAD_DIM:HEAD_DIM + 1, :]
        oT_ref[hh * HEAD_DIM:(hh + 1) * HEAD_DIM, :] = (acc_ref[hh, :HEAD_DIM, :] * inv).astype(BF16)


def _dense_attn(qT, k_tm, vT, *, tq=256, tk=512):
    n_pairs = k_tm.shape[1] // LANES
    s_scratch = pltpu.VMEM((HEADS_PER_PAIR, tk, tq), F32)
    return pl.pallas_call(
        functools.partial(_dense_attn_kernel, tq=tq, tk=tk),
        grid=(n_pairs, SEQ // tq),
        in_specs=[
            pl.BlockSpec((PAIR_ROWS, tq), lambda p, i: (p, i)),
            pl.BlockSpec((SEQ, LANES), lambda p, i: (0, p), pipeline_mode=pl.Buffered(1)),
            pl.BlockSpec((LANES, SEQ), lambda p, i: (p, 0), pipeline_mode=pl.Buffered(1)),
        ],
        out_specs=pl.BlockSpec((PAIR_ROWS, tq), lambda p, i: (p, i)),
        out_shape=jax.ShapeDtypeStruct(qT.shape, BF16),
        scratch_shapes=[
            pltpu.VMEM((HEADS_PER_PAIR, LANES, tq), BF16),
            pltpu.VMEM((HEADS_PER_PAIR, V_AUG_ROWS, tq), F32),
            s_scratch,
            s_scratch,
        ],
        compiler_params=_params("parallel", "arbitrary"),
        name="dense_attn",
    )(qT, k_tm, vT)


class _Band(NamedTuple):
    half_width: int
    dil: int

    @property
    def halo(self):
        return -(-self.half_width // LANES) * LANES


def _write_neg_dist(nd_ref, offset, band, lw, tq):
    rel = (offset + lax.broadcasted_iota(jnp.int32, (lw, tq), 0)
           - lax.broadcasted_iota(jnp.int32, (lw, tq), 1))
    arel = jnp.abs(rel)
    nd = jnp.where(arel <= band.half_width, -arel.astype(F32), -jnp.inf)
    if band.dil > 1:
        nd = jnp.where((rel & (band.dil - 1)) == 0, nd, -jnp.inf)
    nd_ref[...] = nd


def _banded_attn_kernel(hp_ref, qT_ref, k_ref, vT_ref, oT_ref, qpad_ref, *scratch,
                        tq, bands, slot_heads, has_sink, depth):
    i = pl.program_id(0)
    n_heads = qT_ref.shape[0] // HEAD_DIM
    _fill_qpad(qT_ref, qpad_ref, 0, n_heads, tq)
    lws, wstarts, nd_refs, s_bufs, ones = [], [], [], [], []
    for b, band in enumerate(bands):
        nd_ref, s_ref = scratch[2 * b:2 * b + 2]
        lw = tq + 2 * band.halo
        wstart = jnp.clip(i * tq - band.halo, 0, SEQ - lw)
        offset = wstart - i * tq
        prev_offset = jnp.clip((i - 1) * tq - band.halo, 0, SEQ - lw) - (i - 1) * tq
        pl.when((i == 0) | (offset != prev_offset))(
            functools.partial(_write_neg_dist, nd_ref, offset, band, lw, tq))
        lws.append(lw)
        wstarts.append(pl.multiple_of(wstart, LANES))
        nd_refs.append(nd_ref)
        s_bufs.append(s_ref)
        ones.append(_ones_rows(lw))

    def scores(slot):
        m = None
        for b in range(len(bands)):
            head = slot_heads[slot][b]
            pair = head // HEADS_PER_PAIR
            s = jnp.dot(k_ref[pl.ds(wstarts[b], lws[b]), pair * LANES:(pair + 1) * LANES],
                        qpad_ref[head], preferred_element_type=F32)
            s = s + hp_ref[0, head] * nd_refs[b][...]
            s_bufs[b][slot % depth] = s
            mb = jnp.max(s, axis=0, keepdims=True)
            m = mb if m is None else jnp.maximum(m, mb)
        return m

    def finish(slot, m):
        if has_sink:
            sink = hp_ref[1, slot_heads[slot][0]]
            m = jnp.maximum(m, sink)
        acc = None
        for b in range(len(bands)):
            kv = slot_heads[slot][b] // Q_PER_KV
            p = jnp.exp2(s_bufs[b][slot % depth] - m).astype(BF16)
            part = jnp.dot(_v_aug(vT_ref, kv, pl.ds(wstarts[b], lws[b]), ones[b]), p,
                           preferred_element_type=F32)
            acc = part if acc is None else acc + part
        den = acc[HEAD_DIM:HEAD_DIM + 1, :]
        if has_sink:
            den = den + jnp.exp2(sink - m)
        oT_ref[slot * HEAD_DIM:(slot + 1) * HEAD_DIM, :] = (
            acc[:HEAD_DIM, :] * (1.0 / den)).astype(BF16)

    n_slots = len(slot_heads)
    maxes = [scores(slot) for slot in range(depth - 1)]
    for slot in range(n_slots):
        if slot + depth - 1 < n_slots:
            maxes.append(scores(slot + depth - 1))
        finish(slot, maxes[slot])


def _banded_attn(head_params, qT, k_tm, vT, *, bands, slot_heads, has_sink, depth, tq=256):
    n_heads = qT.shape[0] // HEAD_DIM
    scratch = [pltpu.VMEM((n_heads, LANES, tq), BF16)]
    for band in bands:
        lw = tq + 2 * band.halo
        scratch += [pltpu.VMEM((lw, tq), F32), pltpu.VMEM((depth, lw, tq), F32)]
    return pl.pallas_call(
        functools.partial(_banded_attn_kernel, tq=tq, bands=bands, slot_heads=slot_heads,
                          has_sink=has_sink, depth=depth),
        grid=(SEQ // tq,),
        in_specs=[
            pl.BlockSpec(memory_space=pltpu.SMEM),
            pl.BlockSpec((qT.shape[0], tq), lambda i: (0, i)),
            _resident(k_tm.shape),
            _resident(vT.shape),
        ],
        out_specs=pl.BlockSpec((len(slot_heads) * HEAD_DIM, tq), lambda i: (0, i)),
        out_shape=jax.ShapeDtypeStruct((len(slot_heads) * HEAD_DIM, SEQ), BF16),
        scratch_shapes=scratch,
        compiler_params=_params("arbitrary"),
        name="banded_attn",
    )(head_params, qT, k_tm, vT)


def _outproj_kernel(oT_ref, sgT_ref, w_ref, xT_ref, out_ref, *, row_major_out):
    og = oT_ref[...] * sgT_ref[...]
    y = xT_ref[...] + jnp.dot(w_ref[...], og, preferred_element_type=F32)
    out_ref[...] = y.T if row_major_out else y


def _outproj(oT, sgT, w_out, xT, *, row_major_out=False, tm=512):
    width = sgT.shape[0]
    woT = w_out.T.astype(BF16)
    col = lambda i: (0, i)
    if row_major_out:
        out_spec = pl.BlockSpec((tm, D_MODEL), lambda i: (i, 0))
        out_shape = jax.ShapeDtypeStruct((SEQ, D_MODEL), F32)
    else:
        out_spec = pl.BlockSpec((D_MODEL, tm), col)
        out_shape = jax.ShapeDtypeStruct((D_MODEL, SEQ), F32)
    return pl.pallas_call(
        functools.partial(_outproj_kernel, row_major_out=row_major_out),
        grid=(SEQ // tm,),
        in_specs=[
            pl.BlockSpec((width, tm), col),
            pl.BlockSpec((width, tm), col),
            pl.BlockSpec((D_MODEL, width), lambda i: (0, 0)),
            pl.BlockSpec((D_MODEL, tm), col),
        ],
        out_specs=out_spec,
        out_shape=out_shape,
        compiler_params=_params("parallel"),
        name="outproj",
    )(oT, sgT, woT, xT)


def _alibi_slopes(n):
    return jnp.asarray(2.0 ** (-8.0 * np.arange(1, n + 1) / n), dtype=F32)


def _rope_tables():
    t = np.arange(SEQ)
    axis_dim = HEAD_DIM // 2
    freqs = (1.0 / (np.float32(ROPE_THETA) ** (np.arange(0, axis_dim, 2, dtype=np.float32)
                                                / np.float32(axis_dim)))).astype(np.float32)
    row = (t // GRID_W).astype(np.float32)
    col = (t % GRID_W).astype(np.float32)
    ang = np.concatenate([freqs[:, None] * row[None, :], freqs[:, None] * col[None, :]], axis=0)
    ang = ang.astype(np.float32).astype(np.float64)
    return jnp.asarray(np.cos(ang), F32), jnp.asarray(np.sin(ang), F32)


def _mixer_a(xT, norm, w_in, q_gain, k_gain, sink, w_out, *, last):
    qT, k_tm, vT, sgT = _inproj(xT, norm, w_in, q_gain, k_gain, nq=16, nkv=4, gate_w=1024)
    hp = jnp.stack([_alibi_slopes(16), sink.astype(F32)]) * LOG2E
    oT = _banded_attn(hp, qT, k_tm, vT, bands=(_Band(A_WINDOW, 1),),
                      slot_heads=tuple((h,) for h in range(16)), has_sink=True, depth=4)
    return _outproj(oT, sgT, w_out, xT, row_major_out=last)


def _mixer_b(xT, norm, w_in, q_gain, k_gain, w_out):
    qT, k_tm, vT, sgT = _inproj(xT, norm, w_in, q_gain, k_gain, nq=16, nkv=4, gate_w=1024,
                                rope_tables=_rope_tables())
    oT = _dense_attn(qT, k_tm, vT)
    return _outproj(oT, sgT, w_out, xT)


def _mixer_c(xT, norm, w_in, q_gain, k_gain, w_out):
    qT, k_tm, vT, sgT = _inproj(xT, norm, w_in, q_gain, k_gain, nq=24, nkv=6, gate_w=512)
    slopes = _alibi_slopes(24) * LOG2E
    hp = jnp.stack([slopes, jnp.zeros_like(slopes)])
    bands = tuple(_Band(window // 2, dil) for window, dil in C_GROUPS)
    n_slots = HEADS_PER_PAIR
    slot_heads = tuple(tuple(g * n_slots + s for g in range(len(bands))) for s in range(n_slots))
    oT = _banded_attn(hp, qT, k_tm, vT, bands=bands, slot_heads=slot_heads, has_sink=False,
                      depth=4)
    return _outproj(oT, sgT, w_out, xT)


def kernel(x, l0_norm, l0_w_in, l0_q_gain, l0_k_gain, l0_sink, l0_w_out,
           l1_norm, l1_w_in, l1_q_gain, l1_k_gain, l1_w_out,
           l2_norm, l2_w_in, l2_q_gain, l2_k_gain, l2_w_out,
           l3_norm, l3_w_in, l3_q_gain, l3_k_gain, l3_sink, l3_w_out):
    xT = _to_feature_major(x.reshape(SEQ, D_MODEL))
    xT = _mixer_a(xT, l0_norm, l0_w_in, l0_q_gain, l0_k_gain, l0_sink, l0_w_out, last=False)
    xT = _mixer_b(xT, l1_norm, l1_w_in, l1_q_gain, l1_k_gain, l1_w_out)
    xT = _mixer_c(xT, l2_norm, l2_w_in, l2_q_gain, l2_k_gain, l2_w_out)
    out = _mixer_a(xT, l3_norm, l3_w_in, l3_q_gain, l3_k_gain, l3_sink, l3_w_out, last=True)
    return out.reshape(x.shape)
```

```python
import functools
from typing import NamedTuple

import numpy as np
import jax
import jax.numpy as jnp
from jax import lax
from jax.experimental import pallas as pl
from jax.experimental.pallas import tpu as pltpu

D_MODEL = 1024
SEQ = 16384
HEAD_DIM = 64
NORM_EPS = 1e-6
GRID_W = 64
ROPE_THETA = 10000.0
A_WINDOW = 128
C_GROUPS = ((128, 1), (512, 4), (2048, 16))
LOG2E = float(np.log2(np.e))
Q_SCALE = HEAD_DIM ** -0.5 * LOG2E

LANES = 128
BF16_SUBLANES = 16
Q_PER_KV = 4
HEADS_PER_PAIR = 2 * Q_PER_KV
PAIR_ROWS = HEADS_PER_PAIR * HEAD_DIM
V_AUG_ROWS = HEAD_DIM + BF16_SUBLANES
EXP_LIMIT = 64.0
VMEM_LIMIT = 56 * 1024 * 1024

BF16 = jnp.bfloat16
F32 = jnp.float32


def _params(*sem):
    return pltpu.CompilerParams(dimension_semantics=sem, vmem_limit_bytes=VMEM_LIMIT)


def _tile_lanes(x, rep):
    return x if rep == 1 else jnp.concatenate([x] * rep, axis=1)


def _resident(shape):
    return pl.BlockSpec(shape, lambda *_: (0,) * len(shape), pipeline_mode=pl.Buffered(1))


def _x_block(x, tm):
    token_major = x.shape == (SEQ, D_MODEL)
    if token_major:
        return pl.BlockSpec((tm, D_MODEL), lambda i: (i, 0)), True
    return pl.BlockSpec((D_MODEL, tm), lambda i: (0, i)), False


def _inproj_kernel(*refs, nq, nkv, gate_w, rope, x_token_major, tm):
    if rope:
        (x_ref, ng_ref, w_ref, qg_ref, kg_ref, cos_ref, sin_ref,
         qT_ref, k_ref, vT_ref, sgT_ref) = refs
        cos, sin = cos_ref[...], sin_ref[...]
    else:
        x_ref, ng_ref, w_ref, qg_ref, kg_ref, qT_ref, k_ref, vT_ref, sgT_ref = refs
    rep = tm // LANES
    x = x_ref[...].T if x_token_major else x_ref[...]
    r = lax.rsqrt(jnp.mean(x * x, axis=0, keepdims=True) + NORM_EPS)
    h = (x * r * _tile_lanes(ng_ref[...], rep)).astype(BF16)
    qg = _tile_lanes(qg_ref[...], rep)
    kg = _tile_lanes(kg_ref[...], rep)

    def head_norm(ph, gain, scale):
        ss = jnp.sum(ph * ph, axis=0, keepdims=True)
        y = ph * (lax.rsqrt(ss * (1.0 / HEAD_DIM) + NORM_EPS) * scale) * gain
        if rope:
            half = HEAD_DIM // 2
            x1, x2 = y[:half], y[half:]
            y = jnp.concatenate([x1 * cos - x2 * sin, x1 * sin + x2 * cos], axis=0)
        return y

    qw, kw = nq * HEAD_DIM, nkv * HEAD_DIM
    chunk = 256
    for c0 in range(0, qw, chunk):
        pc = jnp.dot(w_ref[c0:c0 + chunk, :], h, preferred_element_type=F32)
        for j in range(chunk // HEAD_DIM):
            y = head_norm(pc[j * HEAD_DIM:(j + 1) * HEAD_DIM], qg, Q_SCALE)
            qT_ref[c0 + j * HEAD_DIM:c0 + (j + 1) * HEAD_DIM, :] = y.astype(BF16)
    pk = jnp.dot(w_ref[qw:qw + kw, :], h, preferred_element_type=F32)
    kn = jnp.concatenate(
        [head_norm(pk[j * HEAD_DIM:(j + 1) * HEAD_DIM], kg, 1.0) for j in range(nkv)], axis=0)
    k_ref[...] = kn.T.astype(BF16)
    pv = jnp.dot(w_ref[qw + kw:qw + 2 * kw, :], h, preferred_element_type=F32)
    vT_ref[...] = pv.astype(BF16)
    g0 = qw + 2 * kw
    for c0 in range(0, gate_w, chunk):
        pg = jnp.dot(w_ref[g0 + c0:g0 + c0 + chunk, :], h, preferred_element_type=F32)
        sgT_ref[c0:c0 + chunk, :] = (pg * (1.0 / (1.0 + jnp.exp(-pg)))).astype(BF16)


def _lane_bcast(v):
    return jnp.broadcast_to(v.astype(F32)[:, None], (v.shape[0], LANES))


def _inproj(xT, norm_gain, w_in, q_gain, k_gain, *, nq, nkv, gate_w, rope_tables=None, tm=512):
    qw, kw = nq * HEAD_DIM, nkv * HEAD_DIM
    in_w = qw + 2 * kw + gate_w
    wT = w_in.T
    rope = rope_tables is not None
    if rope:
        perm = np.concatenate([np.arange(0, HEAD_DIM, 2), np.arange(1, HEAD_DIM, 2)])
        rows = np.arange(in_w)
        nqk = nq + nkv
        rows[:nqk * HEAD_DIM] = (np.arange(nqk)[:, None] * HEAD_DIM + perm[None, :]).reshape(-1)
        wT = wT[rows]
        q_gain, k_gain = q_gain[perm], k_gain[perm]
    wT = wT.astype(BF16)
    const = lambda i: (0, 0)
    col = lambda i: (0, i)
    x_spec, x_token_major = _x_block(xT, tm)
    in_specs = [
        x_spec,
        pl.BlockSpec((D_MODEL, LANES), const),
        pl.BlockSpec((in_w, D_MODEL), const),
        pl.BlockSpec((HEAD_DIM, LANES), const),
        pl.BlockSpec((HEAD_DIM, LANES), const),
    ]
    args = [xT, _lane_bcast(norm_gain), wT, _lane_bcast(q_gain), _lane_bcast(k_gain)]
    if rope:
        in_specs += [pl.BlockSpec((HEAD_DIM // 2, tm), col)] * 2
        args += list(rope_tables)
    return pl.pallas_call(
        functools.partial(_inproj_kernel, nq=nq, nkv=nkv, gate_w=gate_w, rope=rope,
                          x_token_major=x_token_major, tm=tm),
        grid=(SEQ // tm,),
        in_specs=in_specs,
        out_specs=[
            pl.BlockSpec((qw, tm), col),
            pl.BlockSpec((tm, kw), lambda i: (i, 0)),
            pl.BlockSpec((kw, tm), col),
            pl.BlockSpec((gate_w, tm), col),
        ],
        out_shape=[
            jax.ShapeDtypeStruct((qw, SEQ), BF16),
            jax.ShapeDtypeStruct((SEQ, kw), BF16),
            jax.ShapeDtypeStruct((kw, SEQ), BF16),
            jax.ShapeDtypeStruct((gate_w, SEQ), BF16),
        ],
        compiler_params=_params("parallel"),
        name="inproj",
    )(*args)


def _fill_qpad(qT_ref, qpad_ref, head0, n_heads, tq):
    zeros = jnp.zeros((HEAD_DIM, tq), BF16)
    for h in range(n_heads):
        q = qT_ref[h * HEAD_DIM:(h + 1) * HEAD_DIM, :]
        lo, hi = (q, zeros) if ((head0 + h) // Q_PER_KV) % 2 == 0 else (zeros, q)
        qpad_ref[h, :HEAD_DIM, :] = lo
        qpad_ref[h, HEAD_DIM:, :] = hi


def _ones_rows(n):
    row = lax.broadcasted_iota(jnp.int32, (BF16_SUBLANES, n), 0)
    return jnp.where(row == 0, 1.0, 0.0).astype(BF16)


def _v_aug(vT_ref, kv, cols, ones_rows):
    return jnp.concatenate([vT_ref[kv * HEAD_DIM:(kv + 1) * HEAD_DIM, cols], ones_rows], axis=0)


def _two_stage_chunks(n_chunks, produce, consume, carry):
    heads = range(HEADS_PER_PAIR)

    def stage(c, buf, maxes, carry, last=False):
        next_maxes, out = [], []
        for hh in heads:
            if not last:
                next_maxes.append(produce(c + 1, 1 - buf, hh))
            out.append(consume(c, buf, hh, maxes[hh], carry[hh]))
        return tuple(next_maxes), tuple(out)

    def body(jj, state):
        maxes, carry = stage(2 * jj, 0, *state)
        return stage(2 * jj + 1, 1, maxes, carry)

    state = (tuple(produce(0, 0, hh) for hh in heads), carry)
    state = lax.fori_loop(0, n_chunks // 2 - 1, body, state)
    state = stage(n_chunks - 2, 0, *state)
    return stage(n_chunks - 1, 1, *state, last=True)[1]


def _dense_attn_kernel(qT_ref, k_ref, vT_ref, oT_ref, qpad_ref, acc_ref, p0_ref, p1_ref,
                       s0_ref, s1_ref, *, tq, tk):
    n_chunks = SEQ // tk
    p_bufs, s_bufs = (p0_ref, p1_ref), (s0_ref, s1_ref)
    _fill_qpad(qT_ref, qpad_ref, 0, HEADS_PER_PAIR, tq)
    ones_rows = _ones_rows(tk)

    def chunk_scores(c, hh):
        off = pl.multiple_of(c * tk, tk)
        return jnp.dot(k_ref[pl.ds(off, tk), :], qpad_ref[hh], preferred_element_type=F32)

    def chunk_values(c, hh, p):
        off = pl.multiple_of(c * tk, tk)
        return jnp.dot(_v_aug(vT_ref, hh // Q_PER_KV, pl.ds(off, tk), ones_rows), p,
                       preferred_element_type=F32)

    m_ref = tuple(
        jnp.max(jnp.dot(k_ref[0:LANES, :], qpad_ref[hh], preferred_element_type=F32),
                axis=0, keepdims=True) for hh in range(HEADS_PER_PAIR))
    acc_ref[...] = jnp.zeros(acc_ref.shape, F32)

    def fast_produce(c, buf, hh):
        s = chunk_scores(c, hh)
        p_bufs[buf][hh] = jnp.exp2(s - m_ref[hh]).astype(BF16)
        return jnp.max(s, axis=0, keepdims=True)

    def fast_consume(c, buf, hh, chunk_max, top):
        acc_ref[hh] = acc_ref[hh] + chunk_values(c, hh, p_bufs[buf][hh])
        return jnp.maximum(top, chunk_max)

    top = _two_stage_chunks(n_chunks, fast_produce, fast_consume, m_ref)
    excess = functools.reduce(jnp.maximum, [t - r for t, r in zip(top, m_ref)])

    @pl.when(jnp.max(excess) > EXP_LIMIT)
    def _():
        acc_ref[...] = jnp.zeros(acc_ref.shape, F32)

        def safe_produce(c, buf, hh):
            s = chunk_scores(c, hh)
            s_bufs[buf][hh] = s
            return jnp.max(s, axis=0, keepdims=True)

        def safe_consume(c, buf, hh, chunk_max, m):
            mn = jnp.maximum(m, chunk_max)
            p = jnp.exp2(s_bufs[buf][hh] - mn).astype(BF16)
            acc_ref[hh] = jnp.exp2(m - mn) * acc_ref[hh] + chunk_values(c, hh, p)
            return mn

        neg_inf = tuple(jnp.full((1, tq), -jnp.inf, F32) for _ in range(HEADS_PER_PAIR))
        _two_stage_chunks(n_chunks, safe_produce, safe_consume, neg_inf)

    for hh in range(HEADS_PER_PAIR):
        inv = 1.0 / acc_ref[hh, HEAD_DIM:HEAD_DIM + 1, :]
        oT_ref[hh * HEAD_DIM:(hh + 1) * HEAD_DIM, :] = (acc_ref[hh, :HEAD_DIM, :] * inv).astype(BF16)


def _dense_attn(qT, k_tm, vT, *, tq=256, tk=512):
    n_pairs = k_tm.shape[1] // LANES
    p_scratch = pltpu.VMEM((HEADS_PER_PAIR, tk, tq), BF16)
    s_scratch = pltpu.VMEM((HEADS_PER_PAIR, tk, tq), F32)
    return pl.pallas_call(
        functools.partial(_dense_attn_kernel, tq=tq, tk=tk),
        grid=(n_pairs, SEQ // tq),
        in_specs=[
            pl.BlockSpec((PAIR_ROWS, tq), lambda p, i: (p, i)),
            pl.BlockSpec((SEQ, LANES), lambda p, i: (0, p), pipeline_mode=pl.Buffered(1)),
            pl.BlockSpec((LANES, SEQ), lambda p, i: (p, 0), pipeline_mode=pl.Buffered(1)),
        ],
        out_specs=pl.BlockSpec((PAIR_ROWS, tq), lambda p, i: (p, i)),
        out_shape=jax.ShapeDtypeStruct(qT.shape, BF16),
        scratch_shapes=[
            pltpu.VMEM((HEADS_PER_PAIR, LANES, tq), BF16),
            pltpu.VMEM((HEADS_PER_PAIR, V_AUG_ROWS, tq), F32),
            p_scratch,
            p_scratch,
            s_scratch,
            s_scratch,
        ],
        compiler_params=_params("parallel", "arbitrary"),
        name="dense_attn",
    )(qT, k_tm, vT)


class _Band(NamedTuple):
    half_width: int
    dil: int

    @property
    def halo(self):
        return -(-self.half_width // LANES) * LANES


def _write_neg_dist(nd_ref, offset, band, lw, tq):
    rel = (offset + lax.broadcasted_iota(jnp.int32, (lw, tq), 0)
           - lax.broadcasted_iota(jnp.int32, (lw, tq), 1))
    arel = jnp.abs(rel)
    nd = jnp.where(arel <= band.half_width, -arel.astype(F32), -jnp.inf)
    if band.dil > 1:
        nd = jnp.where((rel & (band.dil - 1)) == 0, nd, -jnp.inf)
    nd_ref[...] = nd


def _banded_attn_kernel(hp_ref, qT_ref, k_ref, vT_ref, oT_ref, qpad_ref, *scratch,
                        tq, bands, slot_heads, has_sink, depth):
    i = pl.program_id(0)
    n_heads = qT_ref.shape[0] // HEAD_DIM
    _fill_qpad(qT_ref, qpad_ref, 0, n_heads, tq)
    lws, wstarts, nd_refs, s_bufs, ones = [], [], [], [], []
    for b, band in enumerate(bands):
        nd_ref, s_ref = scratch[2 * b:2 * b + 2]
        lw = tq + 2 * band.halo
        wstart = jnp.clip(i * tq - band.halo, 0, SEQ - lw)
        offset = wstart - i * tq
        prev_offset = jnp.clip((i - 1) * tq - band.halo, 0, SEQ - lw) - (i - 1) * tq
        pl.when((i == 0) | (offset != prev_offset))(
            functools.partial(_write_neg_dist, nd_ref, offset, band, lw, tq))
        lws.append(lw)
        wstarts.append(pl.multiple_of(wstart, LANES))
        nd_refs.append(nd_ref)
        s_bufs.append(s_ref)
        ones.append(_ones_rows(lw))

    def scores(slot):
        m = None
        for b in range(len(bands)):
            head = slot_heads[slot][b]
            pair = head // HEADS_PER_PAIR
            s = jnp.dot(k_ref[pl.ds(wstarts[b], lws[b]), pair * LANES:(pair + 1) * LANES],
                        qpad_ref[head], preferred_element_type=F32)
            s = s + hp_ref[0, head] * nd_refs[b][...]
            s_bufs[b][slot % depth] = s
            mb = jnp.max(s, axis=0, keepdims=True)
            m = mb if m is None else jnp.maximum(m, mb)
        return m

    def finish(slot, m):
        if has_sink:
            sink = hp_ref[1, slot_heads[slot][0]]
            m = jnp.maximum(m, sink)
        acc = None
        for b in range(len(bands)):
            kv = slot_heads[slot][b] // Q_PER_KV
            p = jnp.exp2(s_bufs[b][slot % depth] - m).astype(BF16)
            part = jnp.dot(_v_aug(vT_ref, kv, pl.ds(wstarts[b], lws[b]), ones[b]), p,
                           preferred_element_type=F32)
            acc = part if acc is None else acc + part
        den = acc[HEAD_DIM:HEAD_DIM + 1, :]
        if has_sink:
            den = den + jnp.exp2(sink - m)
        oT_ref[slot * HEAD_DIM:(slot + 1) * HEAD_DIM, :] = (
            acc[:HEAD_DIM, :] * (1.0 / den)).astype(BF16)

    n_slots = len(slot_heads)
    maxes = [scores(slot) for slot in range(depth - 1)]
    for slot in range(n_slots):
        if slot + depth - 1 < n_slots:
            maxes.append(scores(slot + depth - 1))
        finish(slot, maxes[slot])


def _banded_attn(head_params, qT, k_tm, vT, *, bands, slot_heads, has_sink, depth, tq=256):
    n_heads = qT.shape[0] // HEAD_DIM
    scratch = [pltpu.VMEM((n_heads, LANES, tq), BF16)]
    for band in bands:
        lw = tq + 2 * band.halo
        scratch += [pltpu.VMEM((lw, tq), F32), pltpu.VMEM((depth, lw, tq), F32)]
    return pl.pallas_call(
        functools.partial(_banded_attn_kernel, tq=tq, bands=bands, slot_heads=slot_heads,
                          has_sink=has_sink, depth=depth),
        grid=(SEQ // tq,),
        in_specs=[
            pl.BlockSpec(memory_space=pltpu.SMEM),
            pl.BlockSpec((qT.shape[0], tq), lambda i: (0, i)),
            _resident(k_tm.shape),
            _resident(vT.shape),
        ],
        out_specs=pl.BlockSpec((len(slot_heads) * HEAD_DIM, tq), lambda i: (0, i)),
        out_shape=jax.ShapeDtypeStruct((len(slot_heads) * HEAD_DIM, SEQ), BF16),
        scratch_shapes=scratch,
        compiler_params=_params("arbitrary"),
        name="banded_attn",
    )(head_params, qT, k_tm, vT)


def _outproj_kernel(oT_ref, sgT_ref, w_ref, x_ref, out_ref, *, x_token_major, row_major_out):
    og = oT_ref[...] * sgT_ref[...]
    x = x_ref[...].T if x_token_major else x_ref[...]
    y = x + jnp.dot(w_ref[...], og, preferred_element_type=F32)
    out_ref[...] = y.T if row_major_out else y


def _outproj(oT, sgT, w_out, xT, *, row_major_out=False, tm=512):
    width = sgT.shape[0]
    woT = w_out.T.astype(BF16)
    col = lambda i: (0, i)
    x_spec, x_token_major = _x_block(xT, tm)
    if row_major_out:
        out_spec = pl.BlockSpec((tm, D_MODEL), lambda i: (i, 0))
        out_shape = jax.ShapeDtypeStruct((SEQ, D_MODEL), F32)
    else:
        out_spec = pl.BlockSpec((D_MODEL, tm), col)
        out_shape = jax.ShapeDtypeStruct((D_MODEL, SEQ), F32)
    return pl.pallas_call(
        functools.partial(_outproj_kernel, x_token_major=x_token_major,
                          row_major_out=row_major_out),
        grid=(SEQ // tm,),
        in_specs=[
            pl.BlockSpec((width, tm), col),
            pl.BlockSpec((width, tm), col),
            pl.BlockSpec((D_MODEL, width), lambda i: (0, 0)),
            x_spec,
        ],
        out_specs=out_spec,
        out_shape=out_shape,
        compiler_params=_params("parallel"),
        name="outproj",
    )(oT, sgT, woT, xT)


def _alibi_slopes(n):
    return jnp.asarray(2.0 ** (-8.0 * np.arange(1, n + 1) / n), dtype=F32)


def _rope_tables():
    t = np.arange(SEQ)
    axis_dim = HEAD_DIM // 2
    freqs = (1.0 / (np.float32(ROPE_THETA) ** (np.arange(0, axis_dim, 2, dtype=np.float32)
                                                / np.float32(axis_dim)))).astype(np.float32)
    row = (t // GRID_W).astype(np.float32)
    col = (t % GRID_W).astype(np.float32)
    ang = np.concatenate([freqs[:, None] * row[None, :], freqs[:, None] * col[None, :]], axis=0)
    ang = ang.astype(np.float32).astype(np.float64)
    return jnp.asarray(np.cos(ang), F32), jnp.asarray(np.sin(ang), F32)


def _mixer_a(xT, norm, w_in, q_gain, k_gain, sink, w_out, *, last):
    qT, k_tm, vT, sgT = _inproj(xT, norm, w_in, q_gain, k_gain, nq=16, nkv=4, gate_w=1024)
    hp = jnp.stack([_alibi_slopes(16), sink.astype(F32)]) * LOG2E
    oT = _banded_attn(hp, qT, k_tm, vT, bands=(_Band(A_WINDOW, 1),),
                      slot_heads=tuple((h,) for h in range(16)), has_sink=True, depth=4)
    return _outproj(oT, sgT, w_out, xT, row_major_out=last)


def _mixer_b(xT, norm, w_in, q_gain, k_gain, w_out):
    qT, k_tm, vT, sgT = _inproj(xT, norm, w_in, q_gain, k_gain, nq=16, nkv=4, gate_w=1024,
                                rope_tables=_rope_tables())
    oT = _dense_attn(qT, k_tm, vT)
    return _outproj(oT, sgT, w_out, xT)


def _mixer_c(xT, norm, w_in, q_gain, k_gain, w_out):
    qT, k_tm, vT, sgT = _inproj(xT, norm, w_in, q_gain, k_gain, nq=24, nkv=6, gate_w=512)
    slopes = _alibi_slopes(24) * LOG2E
    hp = jnp.stack([slopes, jnp.zeros_like(slopes)])
    bands = tuple(_Band(window // 2, dil) for window, dil in C_GROUPS)
    n_slots = HEADS_PER_PAIR
    slot_heads = tuple(tuple(g * n_slots + s for g in range(len(bands))) for s in range(n_slots))
    oT = _banded_attn(hp, qT, k_tm, vT, bands=bands, slot_heads=slot_heads, has_sink=False,
                      depth=4)
    return _outproj(oT, sgT, w_out, xT)


def kernel(x, l0_norm, l0_w_in, l0_q_gain, l0_k_gain, l0_sink, l0_w_out,
           l1_norm, l1_w_in, l1_q_gain, l1_k_gain, l1_w_out,
           l2_norm, l2_w_in, l2_q_gain, l2_k_gain, l2_w_out,
           l3_norm, l3_w_in, l3_q_gain, l3_k_gain, l3_sink, l3_w_out):
    xT = _mixer_a(x.reshape(SEQ, D_MODEL), l0_norm, l0_w_in, l0_q_gain, l0_k_gain, l0_sink, l0_w_out, last=False)
    xT = _mixer_b(xT, l1_norm, l1_w_in, l1_q_gain, l1_k_gain, l1_w_out)
    xT = _mixer_c(xT, l2_norm, l2_w_in, l2_q_gain, l2_k_gain, l2_w_out)
    out = _mixer_a(xT, l3_norm, l3_w_in, l3_q_gain, l3_k_gain, l3_sink, l3_w_out, last=True)
    return out.reshape(x.shape)
```

```python
import functools
from typing import NamedTuple

import numpy as np
import jax
import jax.numpy as jnp
from jax import lax
from jax.experimental import pallas as pl
from jax.experimental.pallas import tpu as pltpu

D_MODEL = 1024
SEQ = 16384
HEAD_DIM = 64
NORM_EPS = 1e-6
GRID_W = 64
ROPE_THETA = 10000.0
A_WINDOW = 128
C_GROUPS = ((128, 1), (512, 4), (2048, 16))
LOG2E = float(np.log2(np.e))
Q_SCALE = HEAD_DIM ** -0.5 * LOG2E

LANES = 128
BF16_SUBLANES = 16
Q_PER_KV = 4
HEADS_PER_PAIR = 2 * Q_PER_KV
PAIR_ROWS = HEADS_PER_PAIR * HEAD_DIM
V_AUG_ROWS = HEAD_DIM + BF16_SUBLANES
EXP_LIMIT = 64.0
VMEM_LIMIT = 56 * 1024 * 1024

BF16 = jnp.bfloat16
F32 = jnp.float32


def _params(*sem):
    return pltpu.CompilerParams(dimension_semantics=sem, vmem_limit_bytes=VMEM_LIMIT)


def _tile_lanes(x, rep):
    return x if rep == 1 else jnp.concatenate([x] * rep, axis=1)


def _resident(shape):
    return pl.BlockSpec(shape, lambda *_: (0,) * len(shape), pipeline_mode=pl.Buffered(1))


def _x_block(x, tm):
    token_major = x.shape == (SEQ, D_MODEL)
    if token_major:
        return pl.BlockSpec((tm, D_MODEL), lambda i: (i, 0)), True
    return pl.BlockSpec((D_MODEL, tm), lambda i: (0, i)), False


def _inproj_kernel(*refs, nq, nkv, gate_w, rope, x_token_major, tm):
    if rope:
        (x_ref, ng_ref, w_ref, qg_ref, kg_ref, cos_ref, sin_ref,
         qT_ref, k_ref, vT_ref, sgT_ref) = refs
        cos, sin = cos_ref[...], sin_ref[...]
    else:
        x_ref, ng_ref, w_ref, qg_ref, kg_ref, qT_ref, k_ref, vT_ref, sgT_ref = refs
    rep = tm // LANES
    x = x_ref[...].T if x_token_major else x_ref[...]
    r = lax.rsqrt(jnp.mean(x * x, axis=0, keepdims=True) + NORM_EPS)
    h = (x * r * _tile_lanes(ng_ref[...], rep)).astype(BF16)
    qg = _tile_lanes(qg_ref[...], rep)
    kg = _tile_lanes(kg_ref[...], rep)

    def head_norm(ph, gain, scale):
        ss = jnp.sum(ph * ph, axis=0, keepdims=True)
        y = ph * (lax.rsqrt(ss * (1.0 / HEAD_DIM) + NORM_EPS) * scale) * gain
        if rope:
            half = HEAD_DIM // 2
            x1, x2 = y[:half], y[half:]
            y = jnp.concatenate([x1 * cos - x2 * sin, x1 * sin + x2 * cos], axis=0)
        return y

    qw, kw = nq * HEAD_DIM, nkv * HEAD_DIM
    chunk = 256
    for c0 in range(0, qw, chunk):
        pc = jnp.dot(w_ref[c0:c0 + chunk, :], h, preferred_element_type=F32)
        for j in range(chunk // HEAD_DIM):
            y = head_norm(pc[j * HEAD_DIM:(j + 1) * HEAD_DIM], qg, Q_SCALE)
            qT_ref[c0 + j * HEAD_DIM:c0 + (j + 1) * HEAD_DIM, :] = y.astype(BF16)
    pk = jnp.dot(w_ref[qw:qw + kw, :], h, preferred_element_type=F32)
    kn = jnp.concatenate(
        [head_norm(pk[j * HEAD_DIM:(j + 1) * HEAD_DIM], kg, 1.0) for j in range(nkv)], axis=0)
    k_ref[...] = kn.T.astype(BF16)
    pv = jnp.dot(w_ref[qw + kw:qw + 2 * kw, :], h, preferred_element_type=F32)
    vT_ref[...] = pv.astype(BF16)
    g0 = qw + 2 * kw
    for c0 in range(0, gate_w, chunk):
        pg = jnp.dot(w_ref[g0 + c0:g0 + c0 + chunk, :], h, preferred_element_type=F32)
        sgT_ref[c0:c0 + chunk, :] = (pg * (1.0 / (1.0 + jnp.exp(-pg)))).astype(BF16)


def _lane_bcast(v):
    return jnp.broadcast_to(v.astype(F32)[:, None], (v.shape[0], LANES))


def _inproj(xT, norm_gain, w_in, q_gain, k_gain, *, nq, nkv, gate_w, rope_tables=None, tm=512):
    qw, kw = nq * HEAD_DIM, nkv * HEAD_DIM
    in_w = qw + 2 * kw + gate_w
    wT = w_in.T
    rope = rope_tables is not None
    if rope:
        perm = np.concatenate([np.arange(0, HEAD_DIM, 2), np.arange(1, HEAD_DIM, 2)])
        rows = np.arange(in_w)
        nqk = nq + nkv
        rows[:nqk * HEAD_DIM] = (np.arange(nqk)[:, None] * HEAD_DIM + perm[None, :]).reshape(-1)
        wT = wT[rows]
        q_gain, k_gain = q_gain[perm], k_gain[perm]
    wT = wT.astype(BF16)
    const = lambda i: (0, 0)
    col = lambda i: (0, i)
    x_spec, x_token_major = _x_block(xT, tm)
    in_specs = [
        x_spec,
        pl.BlockSpec((D_MODEL, LANES), const),
        pl.BlockSpec((in_w, D_MODEL), const),
        pl.BlockSpec((HEAD_DIM, LANES), const),
        pl.BlockSpec((HEAD_DIM, LANES), const),
    ]
    args = [xT, _lane_bcast(norm_gain), wT, _lane_bcast(q_gain), _lane_bcast(k_gain)]
    if rope:
        in_specs += [pl.BlockSpec((HEAD_DIM // 2, tm), col)] * 2
        args += list(rope_tables)
    return pl.pallas_call(
        functools.partial(_inproj_kernel, nq=nq, nkv=nkv, gate_w=gate_w, rope=rope,
                          x_token_major=x_token_major, tm=tm),
        grid=(SEQ // tm,),
        in_specs=in_specs,
        out_specs=[
            pl.BlockSpec((qw, tm), col),
            pl.BlockSpec((tm, kw), lambda i: (i, 0)),
            pl.BlockSpec((kw, tm), col),
            pl.BlockSpec((gate_w, tm), col),
        ],
        out_shape=[
            jax.ShapeDtypeStruct((qw, SEQ), BF16),
            jax.ShapeDtypeStruct((SEQ, kw), BF16),
            jax.ShapeDtypeStruct((kw, SEQ), BF16),
            jax.ShapeDtypeStruct((gate_w, SEQ), BF16),
        ],
        compiler_params=_params("parallel"),
        name="inproj",
    )(*args)


def _fill_qpad(qT_ref, qpad_ref, head0, n_heads, tq):
    zeros = jnp.zeros((HEAD_DIM, tq), BF16)
    for h in range(n_heads):
        q = qT_ref[h * HEAD_DIM:(h + 1) * HEAD_DIM, :]
        lo, hi = (q, zeros) if ((head0 + h) // Q_PER_KV) % 2 == 0 else (zeros, q)
        qpad_ref[h, :HEAD_DIM, :] = lo
        qpad_ref[h, HEAD_DIM:, :] = hi


def _ones_rows(n):
    row = lax.broadcasted_iota(jnp.int32, (BF16_SUBLANES, n), 0)
    return jnp.where(row == 0, 1.0, 0.0).astype(BF16)


def _v_aug(vT_ref, kv, cols, ones_rows):
    return jnp.concatenate([vT_ref[kv * HEAD_DIM:(kv + 1) * HEAD_DIM, cols], ones_rows], axis=0)


def _two_stage_chunks(n_chunks, produce, consume, carry):
    heads = range(HEADS_PER_PAIR)

    def stage(c, buf, maxes, carry, last=False):
        next_maxes, out = [], []
        for hh in heads:
            if not last:
                next_maxes.append(produce(c + 1, 1 - buf, hh))
            out.append(consume(c, buf, hh, maxes[hh], carry[hh]))
        return tuple(next_maxes), tuple(out)

    def body(jj, state):
        maxes, carry = stage(2 * jj, 0, *state)
        return stage(2 * jj + 1, 1, maxes, carry)

    state = (tuple(produce(0, 0, hh) for hh in heads), carry)
    state = lax.fori_loop(0, n_chunks // 2 - 1, body, state)
    state = stage(n_chunks - 2, 0, *state)
    return stage(n_chunks - 1, 1, *state, last=True)[1]


def _dense_attn_kernel(qT_ref, k_ref, vT_ref, oT_ref, qpad_ref, acc_ref, p0_ref, p1_ref,
                       s0_ref, s1_ref, *, tq, tk):
    n_chunks = SEQ // tk
    p_bufs, s_bufs = (p0_ref, p1_ref), (s0_ref, s1_ref)
    _fill_qpad(qT_ref, qpad_ref, 0, HEADS_PER_PAIR, tq)
    ones_rows = _ones_rows(tk)

    def chunk_scores(c, hh):
        off = pl.multiple_of(c * tk, tk)
        return jnp.dot(k_ref[pl.ds(off, tk), :], qpad_ref[hh], preferred_element_type=F32)

    def chunk_values(c, hh, p):
        off = pl.multiple_of(c * tk, tk)
        return jnp.dot(_v_aug(vT_ref, hh // Q_PER_KV, pl.ds(off, tk), ones_rows), p,
                       preferred_element_type=F32)

    m_ref = tuple(
        jnp.max(jnp.dot(k_ref[0:LANES, :], qpad_ref[hh], preferred_element_type=F32),
                axis=0, keepdims=True) for hh in range(HEADS_PER_PAIR))
    acc_ref[...] = jnp.zeros(acc_ref.shape, F32)

    def fast_produce(c, buf, hh):
        s = chunk_scores(c, hh)
        p_bufs[buf][hh] = jnp.exp2(s - m_ref[hh]).astype(BF16)
        return jnp.max(s, axis=0, keepdims=True)

    def fast_consume(c, buf, hh, chunk_max, top):
        acc_ref[hh] = acc_ref[hh] + chunk_values(c, hh, p_bufs[buf][hh])
        return jnp.maximum(top, chunk_max)

    top = _two_stage_chunks(n_chunks, fast_produce, fast_consume, m_ref)
    excess = functools.reduce(jnp.maximum, [t - r for t, r in zip(top, m_ref)])

    @pl.when(jnp.max(excess) > EXP_LIMIT)
    def _():
        acc_ref[...] = jnp.zeros(acc_ref.shape, F32)

        def safe_produce(c, buf, hh):
            s = chunk_scores(c, hh)
            s_bufs[buf][hh] = s
            return jnp.max(s, axis=0, keepdims=True)

        def safe_consume(c, buf, hh, chunk_max, m):
            mn = jnp.maximum(m, chunk_max)
            p = jnp.exp2(s_bufs[buf][hh] - mn).astype(BF16)
            acc_ref[hh] = jnp.exp2(m - mn) * acc_ref[hh] + chunk_values(c, hh, p)
            return mn

        neg_inf = tuple(jnp.full((1, tq), -jnp.inf, F32) for _ in range(HEADS_PER_PAIR))
        _two_stage_chunks(n_chunks, safe_produce, safe_consume, neg_inf)

    for hh in range(HEADS_PER_PAIR):
        inv = 1.0 / acc_ref[hh, HEAD_DIM:HEAD_DIM + 1, :]
        oT_ref[hh * HEAD_DIM:(hh + 1) * HEAD_DIM, :] = (acc_ref[hh, :HEAD_DIM, :] * inv).astype(BF16)


def _dense_attn(qT, k_tm, vT, *, tq=256, tk=512):
    n_pairs = k_tm.shape[1] // LANES
    p_scratch = pltpu.VMEM((HEADS_PER_PAIR, tk, tq), BF16)
    s_scratch = pltpu.VMEM((HEADS_PER_PAIR, tk, tq), F32)
    return pl.pallas_call(
        functools.partial(_dense_attn_kernel, tq=tq, tk=tk),
        grid=(n_pairs, SEQ // tq),
        in_specs=[
            pl.BlockSpec((PAIR_ROWS, tq), lambda p, i: (p, i)),
            pl.BlockSpec((SEQ, LANES), lambda p, i: (0, p), pipeline_mode=pl.Buffered(1)),
            pl.BlockSpec((LANES, SEQ), lambda p, i: (p, 0), pipeline_mode=pl.Buffered(1)),
        ],
        out_specs=pl.BlockSpec((PAIR_ROWS, tq), lambda p, i: (p, i)),
        out_shape=jax.ShapeDtypeStruct(qT.shape, BF16),
        scratch_shapes=[
            pltpu.VMEM((HEADS_PER_PAIR, LANES, tq), BF16),
            pltpu.VMEM((HEADS_PER_PAIR, V_AUG_ROWS, tq), F32),
            p_scratch,
            p_scratch,
            s_scratch,
            s_scratch,
        ],
        compiler_params=_params("parallel", "arbitrary"),
        name="dense_attn",
    )(qT, k_tm, vT)


class _Band(NamedTuple):
    half_width: int
    dil: int

    @property
    def halo(self):
        return -(-self.half_width // LANES) * LANES


def _write_neg_dist(nd_ref, offset, band, lw, tq, hp_ref=None, bias_ref=None):
    rel = (offset + lax.broadcasted_iota(jnp.int32, (lw, tq), 0)
           - lax.broadcasted_iota(jnp.int32, (lw, tq), 1))
    arel = jnp.abs(rel)
    nd = jnp.where(arel <= band.half_width, -arel.astype(F32), -jnp.inf)
    if band.dil > 1:
        nd = jnp.where((rel & (band.dil - 1)) == 0, nd, -jnp.inf)
    nd_ref[...] = nd
    if bias_ref is not None:
        for head in range(bias_ref.shape[0]):
            bias_ref[head] = hp_ref[0, head] * nd - hp_ref[1, head]


def _banded_attn_kernel(hp_ref, qT_ref, k_ref, vT_ref, oT_ref, qpad_ref, *scratch,
                        tq, bands, slot_heads, has_sink, depth):
    i = pl.program_id(0)
    n_heads = qT_ref.shape[0] // HEAD_DIM
    n_slots = len(slot_heads)
    _fill_qpad(qT_ref, qpad_ref, 0, n_heads, tq)
    if has_sink:
        bias_ref, p_ref = scratch[2 * len(bands):]
    lws, wstarts, nd_refs, s_bufs, ones = [], [], [], [], []
    for b, band in enumerate(bands):
        nd_ref, s_ref = scratch[2 * b:2 * b + 2]
        lw = tq + 2 * band.halo
        wstart = jnp.clip(i * tq - band.halo, 0, SEQ - lw)
        offset = wstart - i * tq
        prev_offset = jnp.clip((i - 1) * tq - band.halo, 0, SEQ - lw) - (i - 1) * tq
        pl.when((i == 0) | (offset != prev_offset))(functools.partial(
            _write_neg_dist, nd_ref, offset, band, lw, tq,
            *((hp_ref, bias_ref) if has_sink else ())))
        lws.append(lw)
        wstarts.append(pl.multiple_of(wstart, LANES))
        nd_refs.append(nd_ref)
        s_bufs.append(s_ref)
        ones.append(_ones_rows(lw))

    def scores(slot):
        m = None
        for b in range(len(bands)):
            head = slot_heads[slot][b]
            pair = head // HEADS_PER_PAIR
            s = jnp.dot(k_ref[pl.ds(wstarts[b], lws[b]), pair * LANES:(pair + 1) * LANES],
                        qpad_ref[head], preferred_element_type=F32)
            s = s + hp_ref[0, head] * nd_refs[b][...]
            s_bufs[b][slot % depth] = s
            mb = jnp.max(s, axis=0, keepdims=True)
            m = mb if m is None else jnp.maximum(m, mb)
        return m

    def finish(slot, m):
        if has_sink:
            sink = hp_ref[1, slot_heads[slot][0]]
            m = jnp.maximum(m, sink)
        acc = None
        for b in range(len(bands)):
            kv = slot_heads[slot][b] // Q_PER_KV
            p = jnp.exp2(s_bufs[b][slot % depth] - m).astype(BF16)
            part = jnp.dot(_v_aug(vT_ref, kv, pl.ds(wstarts[b], lws[b]), ones[b]), p,
                           preferred_element_type=F32)
            acc = part if acc is None else acc + part
        den = acc[HEAD_DIM:HEAD_DIM + 1, :]
        if has_sink:
            den = den + jnp.exp2(sink - m)
        oT_ref[slot * HEAD_DIM:(slot + 1) * HEAD_DIM, :] = (
            acc[:HEAD_DIM, :] * (1.0 / den)).astype(BF16)

    def pipelined(produce, consume):
        produced = [produce(slot) for slot in range(depth - 1)]
        for slot in range(n_slots):
            if slot + depth - 1 < n_slots:
                produced.append(produce(slot + depth - 1))
            consume(slot, produced[slot])
        return produced

    if not has_sink:
        pipelined(scores, finish)
        return

    def fast_scores(slot):
        head = slot_heads[slot][0]
        pair = head // HEADS_PER_PAIR
        t = jnp.dot(k_ref[pl.ds(wstarts[0], lws[0]), pair * LANES:(pair + 1) * LANES],
                    qpad_ref[head], preferred_element_type=F32) + bias_ref[head]
        p_ref[slot % depth] = jnp.exp2(t).astype(BF16)
        return jnp.max(t, axis=0, keepdims=True)

    def fast_finish(slot, _):
        kv = slot_heads[slot][0] // Q_PER_KV
        acc = jnp.dot(_v_aug(vT_ref, kv, pl.ds(wstarts[0], lws[0]), ones[0]),
                      p_ref[slot % depth], preferred_element_type=F32)
        den = acc[HEAD_DIM:HEAD_DIM + 1, :] + 1.0
        oT_ref[slot * HEAD_DIM:(slot + 1) * HEAD_DIM, :] = (
            acc[:HEAD_DIM, :] * (1.0 / den)).astype(BF16)

    excess = functools.reduce(jnp.maximum, pipelined(fast_scores, fast_finish))

    @pl.when(jnp.max(excess) > EXP_LIMIT)
    def _():
        pipelined(scores, finish)


def _banded_attn(head_params, qT, k_tm, vT, *, bands, slot_heads, has_sink, depth, tq=256):
    n_heads = qT.shape[0] // HEAD_DIM
    scratch = [pltpu.VMEM((n_heads, LANES, tq), BF16)]
    for band in bands:
        lw = tq + 2 * band.halo
        scratch += [pltpu.VMEM((lw, tq), F32), pltpu.VMEM((depth, lw, tq), F32)]
    if has_sink:
        assert len(bands) == 1, "the sink-referenced fast path handles one band"
        scratch += [pltpu.VMEM((n_heads, lw, tq), F32), pltpu.VMEM((depth, lw, tq), BF16)]
    return pl.pallas_call(
        functools.partial(_banded_attn_kernel, tq=tq, bands=bands, slot_heads=slot_heads,
                          has_sink=has_sink, depth=depth),
        grid=(SEQ // tq,),
        in_specs=[
            pl.BlockSpec(memory_space=pltpu.SMEM),
            pl.BlockSpec((qT.shape[0], tq), lambda i: (0, i)),
            _resident(k_tm.shape),
            _resident(vT.shape),
        ],
        out_specs=pl.BlockSpec((len(slot_heads) * HEAD_DIM, tq), lambda i: (0, i)),
        out_shape=jax.ShapeDtypeStruct((len(slot_heads) * HEAD_DIM, SEQ), BF16),
        scratch_shapes=scratch,
        compiler_params=_params("arbitrary"),
        name="banded_attn",
    )(head_params, qT, k_tm, vT)


def _outproj_kernel(oT_ref, sgT_ref, w_ref, x_ref, out_ref, *, x_token_major, row_major_out):
    og = oT_ref[...] * sgT_ref[...]
    x = x_ref[...].T if x_token_major else x_ref[...]
    y = x + jnp.dot(w_ref[...], og, preferred_element_type=F32)
    out_ref[...] = y.T if row_major_out else y


def _outproj(oT, sgT, w_out, xT, *, row_major_out=False, tm=512):
    width = sgT.shape[0]
    woT = w_out.T.astype(BF16)
    col = lambda i: (0, i)
    x_spec, x_token_major = _x_block(xT, tm)
    if row_major_out:
        out_spec = pl.BlockSpec((tm, D_MODEL), lambda i: (i, 0))
        out_shape = jax.ShapeDtypeStruct((SEQ, D_MODEL), F32)
    else:
        out_spec = pl.BlockSpec((D_MODEL, tm), col)
        out_shape = jax.ShapeDtypeStruct((D_MODEL, SEQ), F32)
    return pl.pallas_call(
        functools.partial(_outproj_kernel, x_token_major=x_token_major,
                          row_major_out=row_major_out),
        grid=(SEQ // tm,),
        in_specs=[
            pl.BlockSpec((width, tm), col),
            pl.BlockSpec((width, tm), col),
            pl.BlockSpec((D_MODEL, width), lambda i: (0, 0)),
            x_spec,
        ],
        out_specs=out_spec,
        out_shape=out_shape,
        compiler_params=_params("parallel"),
        name="outproj",
    )(oT, sgT, woT, xT)


def _alibi_slopes(n):
    return jnp.asarray(2.0 ** (-8.0 * np.arange(1, n + 1) / n), dtype=F32)


def _rope_tables():
    t = np.arange(SEQ)
    axis_dim = HEAD_DIM // 2
    freqs = (1.0 / (np.float32(ROPE_THETA) ** (np.arange(0, axis_dim, 2, dtype=np.float32)
                                                / np.float32(axis_dim)))).astype(np.float32)
    row = (t // GRID_W).astype(np.float32)
    col = (t % GRID_W).astype(np.float32)
    ang = np.concatenate([freqs[:, None] * row[None, :], freqs[:, None] * col[None, :]], axis=0)
    ang = ang.astype(np.float32).astype(np.float64)
    return jnp.asarray(np.cos(ang), F32), jnp.asarray(np.sin(ang), F32)


def _mixer_a(xT, norm, w_in, q_gain, k_gain, sink, w_out, *, last):
    qT, k_tm, vT, sgT = _inproj(xT, norm, w_in, q_gain, k_gain, nq=16, nkv=4, gate_w=1024)
    hp = jnp.stack([_alibi_slopes(16), sink.astype(F32)]) * LOG2E
    oT = _banded_attn(hp, qT, k_tm, vT, bands=(_Band(A_WINDOW, 1),),
                      slot_heads=tuple((h,) for h in range(16)), has_sink=True, depth=4)
    return _outproj(oT, sgT, w_out, xT, row_major_out=last)


def _mixer_b(xT, norm, w_in, q_gain, k_gain, w_out):
    qT, k_tm, vT, sgT = _inproj(xT, norm, w_in, q_gain, k_gain, nq=16, nkv=4, gate_w=1024,
                                rope_tables=_rope_tables())
    oT = _dense_attn(qT, k_tm, vT)
    return _outproj(oT, sgT, w_out, xT)


def _mixer_c(xT, norm, w_in, q_gain, k_gain, w_out):
    qT, k_tm, vT, sgT = _inproj(xT, norm, w_in, q_gain, k_gain, nq=24, nkv=6, gate_w=512)
    slopes = _alibi_slopes(24) * LOG2E
    hp = jnp.stack([slopes, jnp.zeros_like(slopes)])
    bands = tuple(_Band(window // 2, dil) for window, dil in C_GROUPS)
    n_slots = HEADS_PER_PAIR
    slot_heads = tuple(tuple(g * n_slots + s for g in range(len(bands))) for s in range(n_slots))
    oT = _banded_attn(hp, qT, k_tm, vT, bands=bands, slot_heads=slot_heads, has_sink=False,
                      depth=4)
    return _outproj(oT, sgT, w_out, xT)


def kernel(x, l0_norm, l0_w_in, l0_q_gain, l0_k_gain, l0_sink, l0_w_out,
           l1_norm, l1_w_in, l1_q_gain, l1_k_gain, l1_w_out,
           l2_norm, l2_w_in, l2_q_gain, l2_k_gain, l2_w_out,
           l3_norm, l3_w_in, l3_q_gain, l3_k_gain, l3_sink, l3_w_out):
    xT = _mixer_a(x.reshape(SEQ, D_MODEL), l0_norm, l0_w_in, l0_q_gain, l0_k_gain, l0_sink, l0_w_out, last=False)
    xT = _mixer_b(xT, l1_norm, l1_w_in, l1_q_gain, l1_k_gain, l1_w_out)
    xT = _mixer_c(xT, l2_norm, l2_w_in, l2_q_gain, l2_k_gain, l2_w_out)
    out = _mixer_a(xT, l3_norm, l3_w_in, l3_q_gain, l3_k_gain, l3_sink, l3_w_out, last=True)
    return out.reshape(x.shape)
```

```python
import functools
from typing import NamedTuple

import numpy as np
import jax
import jax.numpy as jnp
from jax import lax
from jax.experimental import pallas as pl
from jax.experimental.pallas import tpu as pltpu

D_MODEL = 1024
SEQ = 16384
HEAD_DIM = 64
NORM_EPS = 1e-6
GRID_W = 64
ROPE_THETA = 10000.0
A_WINDOW = 128
C_GROUPS = ((128, 1), (512, 4), (2048, 16))
LOG2E = float(np.log2(np.e))
Q_SCALE = HEAD_DIM ** -0.5 * LOG2E

LANES = 128
BF16_SUBLANES = 16
Q_PER_KV = 4
HEADS_PER_PAIR = 2 * Q_PER_KV
PAIR_ROWS = HEADS_PER_PAIR * HEAD_DIM
V_AUG_ROWS = HEAD_DIM + BF16_SUBLANES
EXP_LIMIT = 64.0
VMEM_LIMIT = 56 * 1024 * 1024

BF16 = jnp.bfloat16
F32 = jnp.float32


def _params(*sem):
    return pltpu.CompilerParams(dimension_semantics=sem, vmem_limit_bytes=VMEM_LIMIT)


def _tile_lanes(x, rep):
    return x if rep == 1 else jnp.concatenate([x] * rep, axis=1)


def _resident(shape):
    return pl.BlockSpec(shape, lambda *_: (0,) * len(shape), pipeline_mode=pl.Buffered(1))


def _x_block(x, tm):
    token_major = x.shape == (SEQ, D_MODEL)
    if token_major:
        return pl.BlockSpec((tm, D_MODEL), lambda i: (i, 0)), True
    return pl.BlockSpec((D_MODEL, tm), lambda i: (0, i)), False


def _inproj_kernel(*refs, nq, nkv, gate_w, rope, x_token_major, tm):
    if rope:
        (x_ref, ng_ref, w_ref, qg_ref, kg_ref, cos_ref, sin_ref,
         qT_ref, k_ref, vT_ref, sgT_ref) = refs
        cos, sin = cos_ref[...], sin_ref[...]
    else:
        x_ref, ng_ref, w_ref, qg_ref, kg_ref, qT_ref, k_ref, vT_ref, sgT_ref = refs
    rep = tm // LANES
    x = x_ref[...].T if x_token_major else x_ref[...]
    r = lax.rsqrt(jnp.mean(x * x, axis=0, keepdims=True) + NORM_EPS)
    h = (x * r * _tile_lanes(ng_ref[...], rep)).astype(BF16)
    qg = _tile_lanes(qg_ref[...], rep)
    kg = _tile_lanes(kg_ref[...], rep)

    def head_norm(ph, gain, scale):
        ss = jnp.sum(ph * ph, axis=0, keepdims=True)
        y = ph * (lax.rsqrt(ss * (1.0 / HEAD_DIM) + NORM_EPS) * scale) * gain
        if rope:
            half = HEAD_DIM // 2
            x1, x2 = y[:half], y[half:]
            y = jnp.concatenate([x1 * cos - x2 * sin, x1 * sin + x2 * cos], axis=0)
        return y

    qw, kw = nq * HEAD_DIM, nkv * HEAD_DIM
    chunk = 256
    for c0 in range(0, qw, chunk):
        pc = jnp.dot(w_ref[c0:c0 + chunk, :], h, preferred_element_type=F32)
        for j in range(chunk // HEAD_DIM):
            y = head_norm(pc[j * HEAD_DIM:(j + 1) * HEAD_DIM], qg, Q_SCALE)
            qT_ref[c0 + j * HEAD_DIM:c0 + (j + 1) * HEAD_DIM, :] = y.astype(BF16)
    pk = jnp.dot(w_ref[qw:qw + kw, :], h, preferred_element_type=F32)
    kn = jnp.concatenate(
        [head_norm(pk[j * HEAD_DIM:(j + 1) * HEAD_DIM], kg, 1.0) for j in range(nkv)], axis=0)
    k_ref[...] = kn.T.astype(BF16)
    pv = jnp.dot(w_ref[qw + kw:qw + 2 * kw, :], h, preferred_element_type=F32)
    vT_ref[...] = pv.astype(BF16)
    g0 = qw + 2 * kw
    for c0 in range(0, gate_w, chunk):
        pg = jnp.dot(w_ref[g0 + c0:g0 + c0 + chunk, :], h, preferred_element_type=F32)
        sgT_ref[c0:c0 + chunk, :] = (pg * (1.0 / (1.0 + jnp.exp(-pg)))).astype(BF16)


def _lane_bcast(v):
    return jnp.broadcast_to(v.astype(F32)[:, None], (v.shape[0], LANES))


def _inproj(xT, norm_gain, w_in, q_gain, k_gain, *, nq, nkv, gate_w, rope_tables=None, tm=512):
    qw, kw = nq * HEAD_DIM, nkv * HEAD_DIM
    in_w = qw + 2 * kw + gate_w
    wT = w_in.T
    rope = rope_tables is not None
    if rope:
        perm = np.concatenate([np.arange(0, HEAD_DIM, 2), np.arange(1, HEAD_DIM, 2)])
        rows = np.arange(in_w)
        nqk = nq + nkv
        rows[:nqk * HEAD_DIM] = (np.arange(nqk)[:, None] * HEAD_DIM + perm[None, :]).reshape(-1)
        wT = wT[rows]
        q_gain, k_gain = q_gain[perm], k_gain[perm]
    wT = wT.astype(BF16)
    const = lambda i: (0, 0)
    col = lambda i: (0, i)
    x_spec, x_token_major = _x_block(xT, tm)
    in_specs = [
        x_spec,
        pl.BlockSpec((D_MODEL, LANES), const),
        pl.BlockSpec((in_w, D_MODEL), const),
        pl.BlockSpec((HEAD_DIM, LANES), const),
        pl.BlockSpec((HEAD_DIM, LANES), const),
    ]
    args = [xT, _lane_bcast(norm_gain), wT, _lane_bcast(q_gain), _lane_bcast(k_gain)]
    if rope:
        in_specs += [pl.BlockSpec((HEAD_DIM // 2, tm), col)] * 2
        args += list(rope_tables)
    return pl.pallas_call(
        functools.partial(_inproj_kernel, nq=nq, nkv=nkv, gate_w=gate_w, rope=rope,
                          x_token_major=x_token_major, tm=tm),
        grid=(SEQ // tm,),
        in_specs=in_specs,
        out_specs=[
            pl.BlockSpec((qw, tm), col),
            pl.BlockSpec((tm, kw), lambda i: (i, 0)),
            pl.BlockSpec((kw, tm), col),
            pl.BlockSpec((gate_w, tm), col),
        ],
        out_shape=[
            jax.ShapeDtypeStruct((qw, SEQ), BF16),
            jax.ShapeDtypeStruct((SEQ, kw), BF16),
            jax.ShapeDtypeStruct((kw, SEQ), BF16),
            jax.ShapeDtypeStruct((gate_w, SEQ), BF16),
        ],
        compiler_params=_params("parallel"),
        name="inproj",
    )(*args)


def _fill_qpad(qT_ref, qpad_ref, head0, n_heads, tq):
    zeros = jnp.zeros((HEAD_DIM, tq), BF16)
    for h in range(n_heads):
        q = qT_ref[h * HEAD_DIM:(h + 1) * HEAD_DIM, :]
        lo, hi = (q, zeros) if ((head0 + h) // Q_PER_KV) % 2 == 0 else (zeros, q)
        qpad_ref[h, :HEAD_DIM, :] = lo
        qpad_ref[h, HEAD_DIM:, :] = hi


def _ones_rows(n):
    row = lax.broadcasted_iota(jnp.int32, (BF16_SUBLANES, n), 0)
    return jnp.where(row == 0, 1.0, 0.0).astype(BF16)


def _v_aug(vT_ref, kv, cols, ones_rows):
    return jnp.concatenate([vT_ref[kv * HEAD_DIM:(kv + 1) * HEAD_DIM, cols], ones_rows], axis=0)


def _two_stage_chunks(n_chunks, produce, consume, carry, unroll=2):
    heads = range(HEADS_PER_PAIR)
    assert unroll % 2 == 0 and n_chunks % 2 == 0
    n_tail = 2 + (n_chunks - 2) % unroll

    def stage_at(c, buf, maxes, carry, last=False):
        next_maxes, out = [], []
        for hh in heads:
            if not last:
                next_maxes.append(produce(c + 1, 1 - buf, hh))
            out.append(consume(c, buf, hh, maxes[hh], carry[hh]))
        return tuple(next_maxes), tuple(out)

    def body(jj, state):
        for u in range(unroll):
            state = stage_at(unroll * jj + u, u % 2, *state)
        return state

    state = (tuple(produce(0, 0, hh) for hh in heads), carry)
    state = lax.fori_loop(0, (n_chunks - n_tail) // unroll, body, state)
    for c in range(n_chunks - n_tail, n_chunks):
        state = stage_at(c, c % 2, *state, last=c == n_chunks - 1)
    return state[1]


def _dense_attn_kernel(qT_ref, k_ref, vT_ref, oT_ref, qpad_ref, acc_ref, p0_ref, p1_ref,
                       s0_ref, s1_ref, *, tq, tk, unroll):
    n_chunks = SEQ // tk
    p_bufs, s_bufs = (p0_ref, p1_ref), (s0_ref, s1_ref)
    _fill_qpad(qT_ref, qpad_ref, 0, HEADS_PER_PAIR, tq)
    ones_rows = _ones_rows(tk)

    def chunk_scores(c, hh):
        off = pl.multiple_of(c * tk, tk)
        return jnp.dot(k_ref[pl.ds(off, tk), :], qpad_ref[hh], preferred_element_type=F32)

    def chunk_values(c, hh, p):
        off = pl.multiple_of(c * tk, tk)
        return jnp.dot(_v_aug(vT_ref, hh // Q_PER_KV, pl.ds(off, tk), ones_rows), p,
                       preferred_element_type=F32)

    m_ref = tuple(
        jnp.max(jnp.dot(k_ref[0:LANES, :], qpad_ref[hh], preferred_element_type=F32),
                axis=0, keepdims=True) for hh in range(HEADS_PER_PAIR))
    acc_ref[...] = jnp.zeros(acc_ref.shape, F32)

    def fast_produce(c, buf, hh):
        s = chunk_scores(c, hh)
        p_bufs[buf][hh] = jnp.exp2(s - m_ref[hh]).astype(BF16)
        return jnp.max(s, axis=0, keepdims=True)

    def fast_consume(c, buf, hh, chunk_max, top):
        acc_ref[hh] = acc_ref[hh] + chunk_values(c, hh, p_bufs[buf][hh])
        return jnp.maximum(top, chunk_max)

    top = _two_stage_chunks(n_chunks, fast_produce, fast_consume, m_ref, unroll=unroll)
    excess = functools.reduce(jnp.maximum, [t - r for t, r in zip(top, m_ref)])

    @pl.when(jnp.max(excess) > EXP_LIMIT)
    def _():
        acc_ref[...] = jnp.zeros(acc_ref.shape, F32)

        def safe_produce(c, buf, hh):
            s = chunk_scores(c, hh)
            s_bufs[buf][hh] = s
            return jnp.max(s, axis=0, keepdims=True)

        def safe_consume(c, buf, hh, chunk_max, m):
            mn = jnp.maximum(m, chunk_max)
            p = jnp.exp2(s_bufs[buf][hh] - mn).astype(BF16)
            acc_ref[hh] = jnp.exp2(m - mn) * acc_ref[hh] + chunk_values(c, hh, p)
            return mn

        neg_inf = tuple(jnp.full((1, tq), -jnp.inf, F32) for _ in range(HEADS_PER_PAIR))
        _two_stage_chunks(n_chunks, safe_produce, safe_consume, neg_inf)

    for hh in range(HEADS_PER_PAIR):
        inv = 1.0 / acc_ref[hh, HEAD_DIM:HEAD_DIM + 1, :]
        oT_ref[hh * HEAD_DIM:(hh + 1) * HEAD_DIM, :] = (acc_ref[hh, :HEAD_DIM, :] * inv).astype(BF16)


def _dense_attn(qT, k_tm, vT, *, tq=256, tk=512, unroll=6):
    n_pairs = k_tm.shape[1] // LANES
    p_scratch = pltpu.VMEM((HEADS_PER_PAIR, tk, tq), BF16)
    s_scratch = pltpu.VMEM((HEADS_PER_PAIR, tk, tq), F32)
    return pl.pallas_call(
        functools.partial(_dense_attn_kernel, tq=tq, tk=tk, unroll=unroll),
        grid=(n_pairs, SEQ // tq),
        in_specs=[
            pl.BlockSpec((PAIR_ROWS, tq), lambda p, i: (p, i)),
            pl.BlockSpec((SEQ, LANES), lambda p, i: (0, p), pipeline_mode=pl.Buffered(1)),
            pl.BlockSpec((LANES, SEQ), lambda p, i: (p, 0), pipeline_mode=pl.Buffered(1)),
        ],
        out_specs=pl.BlockSpec((PAIR_ROWS, tq), lambda p, i: (p, i)),
        out_shape=jax.ShapeDtypeStruct(qT.shape, BF16),
        scratch_shapes=[
            pltpu.VMEM((HEADS_PER_PAIR, LANES, tq), BF16),
            pltpu.VMEM((HEADS_PER_PAIR, V_AUG_ROWS, tq), F32),
            p_scratch,
            p_scratch,
            s_scratch,
            s_scratch,
        ],
        compiler_params=_params("parallel", "arbitrary"),
        name="dense_attn",
    )(qT, k_tm, vT)


class _Band(NamedTuple):
    half_width: int
    dil: int

    @property
    def halo(self):
        return -(-self.half_width // LANES) * LANES


def _write_neg_dist(nd_ref, offset, band, lw, tq, hp_ref=None, bias_ref=None):
    rel = (offset + lax.broadcasted_iota(jnp.int32, (lw, tq), 0)
           - lax.broadcasted_iota(jnp.int32, (lw, tq), 1))
    arel = jnp.abs(rel)
    nd = jnp.where(arel <= band.half_width, -arel.astype(F32), -jnp.inf)
    if band.dil > 1:
        nd = jnp.where((rel & (band.dil - 1)) == 0, nd, -jnp.inf)
    nd_ref[...] = nd
    if bias_ref is not None:
        for head in range(bias_ref.shape[0]):
            bias_ref[head] = hp_ref[0, head] * nd - hp_ref[1, head]


def _banded_attn_kernel(hp_ref, qT_ref, k_ref, vT_ref, oT_ref, qpad_ref, *scratch,
                        tq, bands, slot_heads, has_sink, depth):
    i = pl.program_id(0)
    n_heads = qT_ref.shape[0] // HEAD_DIM
    n_slots = len(slot_heads)
    _fill_qpad(qT_ref, qpad_ref, 0, n_heads, tq)
    if has_sink:
        bias_ref, p_ref = scratch[2 * len(bands):]
    lws, wstarts, nd_refs, s_bufs, ones = [], [], [], [], []
    for b, band in enumerate(bands):
        nd_ref, s_ref = scratch[2 * b:2 * b + 2]
        lw = tq + 2 * band.halo
        wstart = jnp.clip(i * tq - band.halo, 0, SEQ - lw)
        offset = wstart - i * tq
        prev_offset = jnp.clip((i - 1) * tq - band.halo, 0, SEQ - lw) - (i - 1) * tq
        pl.when((i == 0) | (offset != prev_offset))(functools.partial(
            _write_neg_dist, nd_ref, offset, band, lw, tq,
            *((hp_ref, bias_ref) if has_sink else ())))
        lws.append(lw)
        wstarts.append(pl.multiple_of(wstart, LANES))
        nd_refs.append(nd_ref)
        s_bufs.append(s_ref)
        ones.append(_ones_rows(lw))

    def scores(slot):
        m = None
        for b in range(len(bands)):
            head = slot_heads[slot][b]
            pair = head // HEADS_PER_PAIR
            s = jnp.dot(k_ref[pl.ds(wstarts[b], lws[b]), pair * LANES:(pair + 1) * LANES],
                        qpad_ref[head], preferred_element_type=F32)
            s = s + hp_ref[0, head] * nd_refs[b][...]
            s_bufs[b][slot % depth] = s
            mb = jnp.max(s, axis=0, keepdims=True)
            m = mb if m is None else jnp.maximum(m, mb)
        return m

    def finish(slot, m):
        if has_sink:
            sink = hp_ref[1, slot_heads[slot][0]]
            m = jnp.maximum(m, sink)
        acc = None
        for b in range(len(bands)):
            kv = slot_heads[slot][b] // Q_PER_KV
            p = jnp.exp2(s_bufs[b][slot % depth] - m).astype(BF16)
            part = jnp.dot(_v_aug(vT_ref, kv, pl.ds(wstarts[b], lws[b]), ones[b]), p,
                           preferred_element_type=F32)
            acc = part if acc is None else acc + part
        den = acc[HEAD_DIM:HEAD_DIM + 1, :]
        if has_sink:
            den = den + jnp.exp2(sink - m)
        oT_ref[slot * HEAD_DIM:(slot + 1) * HEAD_DIM, :] = (
            acc[:HEAD_DIM, :] * (1.0 / den)).astype(BF16)

    def pipelined(produce, consume):
        produced = [produce(slot) for slot in range(depth - 1)]
        for slot in range(n_slots):
            if slot + depth - 1 < n_slots:
                produced.append(produce(slot + depth - 1))
            consume(slot, produced[slot])
        return produced

    if not has_sink:
        pipelined(scores, finish)
        return

    def fast_scores(slot):
        head = slot_heads[slot][0]
        pair = head // HEADS_PER_PAIR
        t = jnp.dot(k_ref[pl.ds(wstarts[0], lws[0]), pair * LANES:(pair + 1) * LANES],
                    qpad_ref[head], preferred_element_type=F32) + bias_ref[head]
        p_ref[slot % depth] = jnp.exp2(t).astype(BF16)
        return jnp.max(t, axis=0, keepdims=True)

    def fast_finish(slot, _):
        kv = slot_heads[slot][0] // Q_PER_KV
        acc = jnp.dot(_v_aug(vT_ref, kv, pl.ds(wstarts[0], lws[0]), ones[0]),
                      p_ref[slot % depth], preferred_element_type=F32)
        den = acc[HEAD_DIM:HEAD_DIM + 1, :] + 1.0
        oT_ref[slot * HEAD_DIM:(slot + 1) * HEAD_DIM, :] = (
            acc[:HEAD_DIM, :] * (1.0 / den)).astype(BF16)

    excess = functools.reduce(jnp.maximum, pipelined(fast_scores, fast_finish))

    @pl.when(jnp.max(excess) > EXP_LIMIT)
    def _():
        pipelined(scores, finish)


def _banded_attn(head_params, qT, k_tm, vT, *, bands, slot_heads, has_sink, depth, tq=256):
    n_heads = qT.shape[0] // HEAD_DIM
    scratch = [pltpu.VMEM((n_heads, LANES, tq), BF16)]
    for band in bands:
        lw = tq + 2 * band.halo
        scratch += [pltpu.VMEM((lw, tq), F32), pltpu.VMEM((depth, lw, tq), F32)]
    if has_sink:
        assert len(bands) == 1, "the sink-referenced fast path handles one band"
        scratch += [pltpu.VMEM((n_heads, lw, tq), F32), pltpu.VMEM((depth, lw, tq), BF16)]
    return pl.pallas_call(
        functools.partial(_banded_attn_kernel, tq=tq, bands=bands, slot_heads=slot_heads,
                          has_sink=has_sink, depth=depth),
        grid=(SEQ // tq,),
        in_specs=[
            pl.BlockSpec(memory_space=pltpu.SMEM),
            pl.BlockSpec((qT.shape[0], tq), lambda i: (0, i)),
            _resident(k_tm.shape),
            _resident(vT.shape),
        ],
        out_specs=pl.BlockSpec((len(slot_heads) * HEAD_DIM, tq), lambda i: (0, i)),
        out_shape=jax.ShapeDtypeStruct((len(slot_heads) * HEAD_DIM, SEQ), BF16),
        scratch_shapes=scratch,
        compiler_params=_params("arbitrary"),
        name="banded_attn",
    )(head_params, qT, k_tm, vT)


def _outproj_kernel(oT_ref, sgT_ref, w_ref, x_ref, out_ref, *, x_token_major, row_major_out):
    og = oT_ref[...] * sgT_ref[...]
    x = x_ref[...].T if x_token_major else x_ref[...]
    y = x + jnp.dot(w_ref[...], og, preferred_element_type=F32)
    out_ref[...] = y.T if row_major_out else y


def _outproj(oT, sgT, w_out, xT, *, row_major_out=False, tm=512):
    width = sgT.shape[0]
    woT = w_out.T.astype(BF16)
    col = lambda i: (0, i)
    x_spec, x_token_major = _x_block(xT, tm)
    if row_major_out:
        out_spec = pl.BlockSpec((tm, D_MODEL), lambda i: (i, 0))
        out_shape = jax.ShapeDtypeStruct((SEQ, D_MODEL), F32)
    else:
        out_spec = pl.BlockSpec((D_MODEL, tm), col)
        out_shape = jax.ShapeDtypeStruct((D_MODEL, SEQ), F32)
    return pl.pallas_call(
        functools.partial(_outproj_kernel, x_token_major=x_token_major,
                          row_major_out=row_major_out),
        grid=(SEQ // tm,),
        in_specs=[
            pl.BlockSpec((width, tm), col),
            pl.BlockSpec((width, tm), col),
            pl.BlockSpec((D_MODEL, width), lambda i: (0, 0)),
            x_spec,
        ],
        out_specs=out_spec,
        out_shape=out_shape,
        compiler_params=_params("parallel"),
        name="outproj",
    )(oT, sgT, woT, xT)


def _alibi_slopes(n):
    return jnp.asarray(2.0 ** (-8.0 * np.arange(1, n + 1) / n), dtype=F32)


def _rope_tables():
    t = np.arange(SEQ)
    axis_dim = HEAD_DIM // 2
    freqs = (1.0 / (np.float32(ROPE_THETA) ** (np.arange(0, axis_dim, 2, dtype=np.float32)
                                                / np.float32(axis_dim)))).astype(np.float32)
    row = (t // GRID_W).astype(np.float32)
    col = (t % GRID_W).astype(np.float32)
    ang = np.concatenate([freqs[:, None] * row[None, :], freqs[:, None] * col[None, :]], axis=0)
    ang = ang.astype(np.float32).astype(np.float64)
    return jnp.asarray(np.cos(ang), F32), jnp.asarray(np.sin(ang), F32)


def _mixer_a(xT, norm, w_in, q_gain, k_gain, sink, w_out, *, last):
    qT, k_tm, vT, sgT = _inproj(xT, norm, w_in, q_gain, k_gain, nq=16, nkv=4, gate_w=1024)
    hp = jnp.stack([_alibi_slopes(16), sink.astype(F32)]) * LOG2E
    oT = _banded_attn(hp, qT, k_tm, vT, bands=(_Band(A_WINDOW, 1),),
                      slot_heads=tuple((h,) for h in range(16)), has_sink=True, depth=4)
    return _outproj(oT, sgT, w_out, xT, row_major_out=last)


def _mixer_b(xT, norm, w_in, q_gain, k_gain, w_out):
    qT, k_tm, vT, sgT = _inproj(xT, norm, w_in, q_gain, k_gain, nq=16, nkv=4, gate_w=1024,
                                rope_tables=_rope_tables())
    oT = _dense_attn(qT, k_tm, vT)
    return _outproj(oT, sgT, w_out, xT)


def _mixer_c(xT, norm, w_in, q_gain, k_gain, w_out):
    qT, k_tm, vT, sgT = _inproj(xT, norm, w_in, q_gain, k_gain, nq=24, nkv=6, gate_w=512)
    slopes = _alibi_slopes(24) * LOG2E
    hp = jnp.stack([slopes, jnp.zeros_like(slopes)])
    bands = tuple(_Band(window // 2, dil) for window, dil in C_GROUPS)
    n_slots = HEADS_PER_PAIR
    slot_heads = tuple(tuple(g * n_slots + s for g in range(len(bands))) for s in range(n_slots))
    oT = _banded_attn(hp, qT, k_tm, vT, bands=bands, slot_heads=slot_heads, has_sink=False,
                      depth=4)
    return _outproj(oT, sgT, w_out, xT)


def kernel(x, l0_norm, l0_w_in, l0_q_gain, l0_k_gain, l0_sink, l0_w_out,
           l1_norm, l1_w_in, l1_q_gain, l1_k_gain, l1_w_out,
           l2_norm, l2_w_in, l2_q_gain, l2_k_gain, l2_w_out,
           l3_norm, l3_w_in, l3_q_gain, l3_k_gain, l3_sink, l3_w_out):
    xT = _mixer_a(x.reshape(SEQ, D_MODEL), l0_norm, l0_w_in, l0_q_gain, l0_k_gain, l0_sink, l0_w_out, last=False)
    xT = _mixer_b(xT, l1_norm, l1_w_in, l1_q_gain, l1_k_gain, l1_w_out)
    xT = _mixer_c(xT, l2_norm, l2_w_in, l2_q_gain, l2_k_gain, l2_w_out)
    out = _mixer_a(xT, l3_norm, l3_w_in, l3_q_gain, l3_k_gain, l3_sink, l3_w_out, last=True)
    return out.reshape(x.shape)
```

```python
import functools
from typing import NamedTuple

import numpy as np
import jax
import jax.numpy as jnp
from jax import lax
from jax.experimental import pallas as pl
from jax.experimental.pallas import tpu as pltpu

D_MODEL = 1024
SEQ = 16384
HEAD_DIM = 64
NORM_EPS = 1e-6
GRID_W = 64
ROPE_THETA = 10000.0
A_WINDOW = 128
C_GROUPS = ((128, 1), (512, 4), (2048, 16))
LOG2E = float(np.log2(np.e))
Q_SCALE = HEAD_DIM ** -0.5 * LOG2E

LANES = 128
BF16_SUBLANES = 16
Q_PER_KV = 4
HEADS_PER_PAIR = 2 * Q_PER_KV
PAIR_ROWS = HEADS_PER_PAIR * HEAD_DIM
V_AUG_ROWS = HEAD_DIM + BF16_SUBLANES
EXP_LIMIT = 64.0
VMEM_LIMIT = 56 * 1024 * 1024

BF16 = jnp.bfloat16
F32 = jnp.float32


def _params(*sem):
    return pltpu.CompilerParams(dimension_semantics=sem, vmem_limit_bytes=VMEM_LIMIT)


def _tile_lanes(x, rep):
    return x if rep == 1 else jnp.concatenate([x] * rep, axis=1)


def _resident(shape):
    return pl.BlockSpec(shape, lambda *_: (0,) * len(shape), pipeline_mode=pl.Buffered(1))


def _x_block(x, tm):
    token_major = x.shape == (SEQ, D_MODEL)
    if token_major:
        return pl.BlockSpec((tm, D_MODEL), lambda i: (i, 0)), True
    return pl.BlockSpec((D_MODEL, tm), lambda i: (0, i)), False


def _inproj_kernel(*refs, nq, nkv, gate_w, rope, x_token_major, tm):
    if rope:
        (x_ref, ng_ref, w_ref, qg_ref, kg_ref, cos_ref, sin_ref,
         qT_ref, k_ref, vT_ref, sgT_ref) = refs
        cos, sin = cos_ref[...], sin_ref[...]
    else:
        x_ref, ng_ref, w_ref, qg_ref, kg_ref, qT_ref, k_ref, vT_ref, sgT_ref = refs
    rep = tm // LANES
    x = x_ref[...].T if x_token_major else x_ref[...]
    r = lax.rsqrt(jnp.mean(x * x, axis=0, keepdims=True) + NORM_EPS)
    h = (x * r * _tile_lanes(ng_ref[...], rep)).astype(BF16)
    qg = _tile_lanes(qg_ref[...], rep)
    kg = _tile_lanes(kg_ref[...], rep)

    def head_norm(ph, gain, scale):
        ss = jnp.sum(ph * ph, axis=0, keepdims=True)
        y = ph * (lax.rsqrt(ss * (1.0 / HEAD_DIM) + NORM_EPS) * scale) * gain
        if rope:
            half = HEAD_DIM // 2
            x1, x2 = y[:half], y[half:]
            y = jnp.concatenate([x1 * cos - x2 * sin, x1 * sin + x2 * cos], axis=0)
        return y

    qw, kw = nq * HEAD_DIM, nkv * HEAD_DIM
    chunk = 256
    for c0 in range(0, qw, chunk):
        pc = jnp.dot(w_ref[c0:c0 + chunk, :], h, preferred_element_type=F32)
        for j in range(chunk // HEAD_DIM):
            y = head_norm(pc[j * HEAD_DIM:(j + 1) * HEAD_DIM], qg, Q_SCALE)
            qT_ref[c0 + j * HEAD_DIM:c0 + (j + 1) * HEAD_DIM, :] = y.astype(BF16)
    pk = jnp.dot(w_ref[qw:qw + kw, :], h, preferred_element_type=F32)
    kn = jnp.concatenate(
        [head_norm(pk[j * HEAD_DIM:(j + 1) * HEAD_DIM], kg, 1.0) for j in range(nkv)], axis=0)
    k_ref[...] = kn.T.astype(BF16)
    pv = jnp.dot(w_ref[qw + kw:qw + 2 * kw, :], h, preferred_element_type=F32)
    vT_ref[...] = pv.astype(BF16)
    g0 = qw + 2 * kw
    for c0 in range(0, gate_w, chunk):
        pg = jnp.dot(w_ref[g0 + c0:g0 + c0 + chunk, :], h, preferred_element_type=F32)
        sgT_ref[c0:c0 + chunk, :] = (pg * (1.0 / (1.0 + jnp.exp(-pg)))).astype(BF16)


def _lane_bcast(v):
    return jnp.broadcast_to(v.astype(F32)[:, None], (v.shape[0], LANES))


def _inproj(xT, norm_gain, w_in, q_gain, k_gain, *, nq, nkv, gate_w, rope_tables=None, tm=512):
    qw, kw = nq * HEAD_DIM, nkv * HEAD_DIM
    in_w = qw + 2 * kw + gate_w
    wT = w_in.T
    rope = rope_tables is not None
    if rope:
        perm = np.concatenate([np.arange(0, HEAD_DIM, 2), np.arange(1, HEAD_DIM, 2)])
        rows = np.arange(in_w)
        nqk = nq + nkv
        rows[:nqk * HEAD_DIM] = (np.arange(nqk)[:, None] * HEAD_DIM + perm[None, :]).reshape(-1)
        wT = wT[rows]
        q_gain, k_gain = q_gain[perm], k_gain[perm]
    wT = wT.astype(BF16)
    const = lambda i: (0, 0)
    col = lambda i: (0, i)
    x_spec, x_token_major = _x_block(xT, tm)
    in_specs = [
        x_spec,
        pl.BlockSpec((D_MODEL, LANES), const),
        pl.BlockSpec((in_w, D_MODEL), const),
        pl.BlockSpec((HEAD_DIM, LANES), const),
        pl.BlockSpec((HEAD_DIM, LANES), const),
    ]
    args = [xT, _lane_bcast(norm_gain), wT, _lane_bcast(q_gain), _lane_bcast(k_gain)]
    if rope:
        in_specs += [pl.BlockSpec((HEAD_DIM // 2, tm), col)] * 2
        args += list(rope_tables)
    return pl.pallas_call(
        functools.partial(_inproj_kernel, nq=nq, nkv=nkv, gate_w=gate_w, rope=rope,
                          x_token_major=x_token_major, tm=tm),
        grid=(SEQ // tm,),
        in_specs=in_specs,
        out_specs=[
            pl.BlockSpec((qw, tm), col),
            pl.BlockSpec((tm, kw), lambda i: (i, 0)),
            pl.BlockSpec((kw, tm), col),
            pl.BlockSpec((gate_w, tm), col),
        ],
        out_shape=[
            jax.ShapeDtypeStruct((qw, SEQ), BF16),
            jax.ShapeDtypeStruct((SEQ, kw), BF16),
            jax.ShapeDtypeStruct((kw, SEQ), BF16),
            jax.ShapeDtypeStruct((gate_w, SEQ), BF16),
        ],
        compiler_params=_params("parallel"),
        name="inproj",
    )(*args)


def _fill_qpad(qT_ref, qpad_ref, head0, n_heads, tq):
    zeros = jnp.zeros((HEAD_DIM, tq), BF16)
    for h in range(n_heads):
        q = qT_ref[h * HEAD_DIM:(h + 1) * HEAD_DIM, :]
        lo, hi = (q, zeros) if ((head0 + h) // Q_PER_KV) % 2 == 0 else (zeros, q)
        qpad_ref[h, :HEAD_DIM, :] = lo
        qpad_ref[h, HEAD_DIM:, :] = hi


def _ones_rows(n):
    row = lax.broadcasted_iota(jnp.int32, (BF16_SUBLANES, n), 0)
    return jnp.where(row == 0, 1.0, 0.0).astype(BF16)


def _v_aug(vT_ref, kv, cols, ones_rows):
    return jnp.concatenate([vT_ref[kv * HEAD_DIM:(kv + 1) * HEAD_DIM, cols], ones_rows], axis=0)


def _two_stage_chunks(n_chunks, produce, consume, carry, unroll=2):
    heads = range(HEADS_PER_PAIR)
    assert unroll % 2 == 0 and n_chunks % 2 == 0
    n_tail = 2 + (n_chunks - 2) % unroll

    def stage_at(c, buf, maxes, carry, last=False):
        next_maxes, out = [], []
        for hh in heads:
            if not last:
                next_maxes.append(produce(c + 1, 1 - buf, hh))
            out.append(consume(c, buf, hh, maxes[hh], carry[hh]))
        return tuple(next_maxes), tuple(out)

    def body(jj, state):
        for u in range(unroll):
            state = stage_at(unroll * jj + u, u % 2, *state)
        return state

    state = (tuple(produce(0, 0, hh) for hh in heads), carry)
    state = lax.fori_loop(0, (n_chunks - n_tail) // unroll, body, state)
    for c in range(n_chunks - n_tail, n_chunks):
        state = stage_at(c, c % 2, *state, last=c == n_chunks - 1)
    return state[1]


def _dense_attn_kernel(qT_ref, k_ref, vT_ref, oT_ref, qpad_ref, acc_ref, p0_ref, p1_ref,
                       s0_ref, s1_ref, *, tq, tk, unroll):
    n_chunks = SEQ // tk
    p_bufs, s_bufs = (p0_ref, p1_ref), (s0_ref, s1_ref)
    _fill_qpad(qT_ref, qpad_ref, 0, HEADS_PER_PAIR, tq)
    ones_rows = _ones_rows(tk)

    def chunk_scores(c, hh):
        off = pl.multiple_of(c * tk, tk)
        return jnp.dot(k_ref[pl.ds(off, tk), :], qpad_ref[hh], preferred_element_type=F32)

    def chunk_values(c, hh, p):
        off = pl.multiple_of(c * tk, tk)
        return jnp.dot(_v_aug(vT_ref, hh // Q_PER_KV, pl.ds(off, tk), ones_rows), p,
                       preferred_element_type=F32)

    m_ref = tuple(
        jnp.max(jnp.dot(k_ref[0:LANES, :], qpad_ref[hh], preferred_element_type=F32),
                axis=0, keepdims=True) for hh in range(HEADS_PER_PAIR))
    acc_ref[...] = jnp.zeros(acc_ref.shape, F32)

    def fast_produce(c, buf, hh):
        s = chunk_scores(c, hh)
        p_bufs[buf][hh] = jnp.exp2(s - m_ref[hh]).astype(BF16)
        return jnp.max(s, axis=0, keepdims=True)

    def fast_consume(c, buf, hh, chunk_max, top):
        acc_ref[hh] = acc_ref[hh] + chunk_values(c, hh, p_bufs[buf][hh])
        return jnp.maximum(top, chunk_max)

    top = _two_stage_chunks(n_chunks, fast_produce, fast_consume, m_ref, unroll=unroll)
    excess = functools.reduce(jnp.maximum, [t - r for t, r in zip(top, m_ref)])

    @pl.when(jnp.max(excess) > EXP_LIMIT)
    def _():
        acc_ref[...] = jnp.zeros(acc_ref.shape, F32)

        def safe_produce(c, buf, hh):
            s = chunk_scores(c, hh)
            s_bufs[buf][hh] = s
            return jnp.max(s, axis=0, keepdims=True)

        def safe_consume(c, buf, hh, chunk_max, m):
            mn = jnp.maximum(m, chunk_max)
            p = jnp.exp2(s_bufs[buf][hh] - mn).astype(BF16)
            acc_ref[hh] = jnp.exp2(m - mn) * acc_ref[hh] + chunk_values(c, hh, p)
            return mn

        neg_inf = tuple(jnp.full((1, tq), -jnp.inf, F32) for _ in range(HEADS_PER_PAIR))
        _two_stage_chunks(n_chunks, safe_produce, safe_consume, neg_inf)

    for hh in range(HEADS_PER_PAIR):
        inv = 1.0 / acc_ref[hh, HEAD_DIM:HEAD_DIM + 1, :]
        oT_ref[hh * HEAD_DIM:(hh + 1) * HEAD_DIM, :] = (acc_ref[hh, :HEAD_DIM, :] * inv).astype(BF16)


def _dense_attn(qT, k_tm, vT, *, tq=256, tk=512, unroll=10):
    n_pairs = k_tm.shape[1] // LANES
    p_scratch = pltpu.VMEM((HEADS_PER_PAIR, tk, tq), BF16)
    s_scratch = pltpu.VMEM((HEADS_PER_PAIR, tk, tq), F32)
    return pl.pallas_call(
        functools.partial(_dense_attn_kernel, tq=tq, tk=tk, unroll=unroll),
        grid=(n_pairs, SEQ // tq),
        in_specs=[
            pl.BlockSpec((PAIR_ROWS, tq), lambda p, i: (p, i)),
            pl.BlockSpec((SEQ, LANES), lambda p, i: (0, p), pipeline_mode=pl.Buffered(1)),
            pl.BlockSpec((LANES, SEQ), lambda p, i: (p, 0), pipeline_mode=pl.Buffered(1)),
        ],
        out_specs=pl.BlockSpec((PAIR_ROWS, tq), lambda p, i: (p, i)),
        out_shape=jax.ShapeDtypeStruct(qT.shape, BF16),
        scratch_shapes=[
            pltpu.VMEM((HEADS_PER_PAIR, LANES, tq), BF16),
            pltpu.VMEM((HEADS_PER_PAIR, V_AUG_ROWS, tq), F32),
            p_scratch,
            p_scratch,
            s_scratch,
            s_scratch,
        ],
        compiler_params=_params("parallel", "arbitrary"),
        name="dense_attn",
    )(qT, k_tm, vT)


class _Band(NamedTuple):
    half_width: int
    dil: int

    @property
    def halo(self):
        return -(-self.half_width // LANES) * LANES


def _write_neg_dist(nd_ref, offset, band, lw, tq, hp_ref=None, bias_ref=None):
    rel = (offset + lax.broadcasted_iota(jnp.int32, (lw, tq), 0)
           - lax.broadcasted_iota(jnp.int32, (lw, tq), 1))
    arel = jnp.abs(rel)
    nd = jnp.where(arel <= band.half_width, -arel.astype(F32), -jnp.inf)
    if band.dil > 1:
        nd = jnp.where((rel & (band.dil - 1)) == 0, nd, -jnp.inf)
    nd_ref[...] = nd
    if bias_ref is not None:
        for head in range(bias_ref.shape[0]):
            bias_ref[head] = hp_ref[0, head] * nd - hp_ref[1, head]


def _banded_attn_kernel(hp_ref, qT_ref, k_ref, vT_ref, sgT_ref, w_ref, x_ref, out_ref,
                        qpad_ref, oT_ref, *scratch,
                        tq, bands, slot_heads, has_sink, depth, x_token_major, row_major_out):
    i = pl.program_id(0)
    n_heads = qT_ref.shape[0] // HEAD_DIM
    n_slots = len(slot_heads)
    _fill_qpad(qT_ref, qpad_ref, 0, n_heads, tq)
    if has_sink:
        bias_ref, p_ref = scratch[2 * len(bands):]
    lws, wstarts, nd_refs, s_bufs, ones = [], [], [], [], []
    for b, band in enumerate(bands):
        nd_ref, s_ref = scratch[2 * b:2 * b + 2]
        lw = tq + 2 * band.halo
        wstart = jnp.clip(i * tq - band.halo, 0, SEQ - lw)
        offset = wstart - i * tq
        prev_offset = jnp.clip((i - 1) * tq - band.halo, 0, SEQ - lw) - (i - 1) * tq
        pl.when((i == 0) | (offset != prev_offset))(functools.partial(
            _write_neg_dist, nd_ref, offset, band, lw, tq,
            *((hp_ref, bias_ref) if has_sink else ())))
        lws.append(lw)
        wstarts.append(pl.multiple_of(wstart, LANES))
        nd_refs.append(nd_ref)
        s_bufs.append(s_ref)
        ones.append(_ones_rows(lw))

    def scores(slot):
        m = None
        for b in range(len(bands)):
            head = slot_heads[slot][b]
            pair = head // HEADS_PER_PAIR
            s = jnp.dot(k_ref[pl.ds(wstarts[b], lws[b]), pair * LANES:(pair + 1) * LANES],
                        qpad_ref[head], preferred_element_type=F32)
            s = s + hp_ref[0, head] * nd_refs[b][...]
            s_bufs[b][slot % depth] = s
            mb = jnp.max(s, axis=0, keepdims=True)
            m = mb if m is None else jnp.maximum(m, mb)
        return m

    def finish(slot, m):
        if has_sink:
            sink = hp_ref[1, slot_heads[slot][0]]
            m = jnp.maximum(m, sink)
        acc = None
        for b in range(len(bands)):
            kv = slot_heads[slot][b] // Q_PER_KV
            p = jnp.exp2(s_bufs[b][slot % depth] - m).astype(BF16)
            part = jnp.dot(_v_aug(vT_ref, kv, pl.ds(wstarts[b], lws[b]), ones[b]), p,
                           preferred_element_type=F32)
            acc = part if acc is None else acc + part
        den = acc[HEAD_DIM:HEAD_DIM + 1, :]
        if has_sink:
            den = den + jnp.exp2(sink - m)
        oT_ref[slot * HEAD_DIM:(slot + 1) * HEAD_DIM, :] = (
            acc[:HEAD_DIM, :] * (1.0 / den)).astype(BF16)

    def pipelined(produce, consume):
        produced = [produce(slot) for slot in range(depth - 1)]
        for slot in range(n_slots):
            if slot + depth - 1 < n_slots:
                produced.append(produce(slot + depth - 1))
            consume(slot, produced[slot])
        return produced

    def out_projection():
        og = oT_ref[...] * sgT_ref[...]
        x = x_ref[...].T if x_token_major else x_ref[...]
        y = x + jnp.dot(w_ref[...], og, preferred_element_type=F32)
        out_ref[...] = y.T if row_major_out else y

    if not has_sink:
        pipelined(scores, finish)
        out_projection()
        return

    def fast_scores(slot):
        head = slot_heads[slot][0]
        pair = head // HEADS_PER_PAIR
        t = jnp.dot(k_ref[pl.ds(wstarts[0], lws[0]), pair * LANES:(pair + 1) * LANES],
                    qpad_ref[head], preferred_element_type=F32) + bias_ref[head]
        p_ref[slot % depth] = jnp.exp2(t).astype(BF16)
        return jnp.max(t, axis=0, keepdims=True)

    def fast_finish(slot, _):
        kv = slot_heads[slot][0] // Q_PER_KV
        acc = jnp.dot(_v_aug(vT_ref, kv, pl.ds(wstarts[0], lws[0]), ones[0]),
                      p_ref[slot % depth], preferred_element_type=F32)
        den = acc[HEAD_DIM:HEAD_DIM + 1, :] + 1.0
        oT_ref[slot * HEAD_DIM:(slot + 1) * HEAD_DIM, :] = (
            acc[:HEAD_DIM, :] * (1.0 / den)).astype(BF16)

    excess = functools.reduce(jnp.maximum, pipelined(fast_scores, fast_finish))

    @pl.when(jnp.max(excess) > EXP_LIMIT)
    def _():
        pipelined(scores, finish)

    out_projection()


def _residual_out(row_major_out, tm):
    if row_major_out:
        return (pl.BlockSpec((tm, D_MODEL), lambda i: (i, 0)),
                jax.ShapeDtypeStruct((SEQ, D_MODEL), F32))
    return (pl.BlockSpec((D_MODEL, tm), lambda i: (0, i)),
            jax.ShapeDtypeStruct((D_MODEL, SEQ), F32))


def _banded_mixer(head_params, qT, k_tm, vT, sgT, w_out, xT, *, bands, slot_heads, has_sink,
                  depth, row_major_out=False, tq=256):
    n_heads = qT.shape[0] // HEAD_DIM
    width = len(slot_heads) * HEAD_DIM
    col = lambda i: (0, i)
    scratch = [pltpu.VMEM((n_heads, LANES, tq), BF16), pltpu.VMEM((width, tq), BF16)]
    for band in bands:
        lw = tq + 2 * band.halo
        scratch += [pltpu.VMEM((lw, tq), F32), pltpu.VMEM((depth, lw, tq), F32)]
    if has_sink:
        assert len(bands) == 1, "the sink-referenced fast path handles one band"
        scratch += [pltpu.VMEM((n_heads, lw, tq), F32), pltpu.VMEM((depth, lw, tq), BF16)]
    x_spec, x_token_major = _x_block(xT, tq)
    out_spec, out_shape = _residual_out(row_major_out, tq)
    woT = w_out.T.astype(BF16)
    return pl.pallas_call(
        functools.partial(_banded_attn_kernel, tq=tq, bands=bands, slot_heads=slot_heads,
                          has_sink=has_sink, depth=depth, x_token_major=x_token_major,
                          row_major_out=row_major_out),
        grid=(SEQ // tq,),
        in_specs=[
            pl.BlockSpec(memory_space=pltpu.SMEM),
            pl.BlockSpec((qT.shape[0], tq), col),
            _resident(k_tm.shape),
            _resident(vT.shape),
            pl.BlockSpec((width, tq), col),
            _resident(woT.shape),
            x_spec,
        ],
        out_specs=out_spec,
        out_shape=out_shape,
        scratch_shapes=scratch,
        compiler_params=_params("arbitrary"),
        name="banded_mixer",
    )(head_params, qT, k_tm, vT, sgT, woT, xT)


def _outproj_kernel(oT_ref, sgT_ref, w_ref, x_ref, out_ref, *, x_token_major, row_major_out):
    og = oT_ref[...] * sgT_ref[...]
    x = x_ref[...].T if x_token_major else x_ref[...]
    y = x + jnp.dot(w_ref[...], og, preferred_element_type=F32)
    out_ref[...] = y.T if row_major_out else y


def _outproj(oT, sgT, w_out, xT, *, row_major_out=False, tm=512):
    width = sgT.shape[0]
    woT = w_out.T.astype(BF16)
    col = lambda i: (0, i)
    x_spec, x_token_major = _x_block(xT, tm)
    out_spec, out_shape = _residual_out(row_major_out, tm)
    return pl.pallas_call(
        functools.partial(_outproj_kernel, x_token_major=x_token_major,
                          row_major_out=row_major_out),
        grid=(SEQ // tm,),
        in_specs=[
            pl.BlockSpec((width, tm), col),
            pl.BlockSpec((width, tm), col),
            pl.BlockSpec((D_MODEL, width), lambda i: (0, 0)),
            x_spec,
        ],
        out_specs=out_spec,
        out_shape=out_shape,
        compiler_params=_params("parallel"),
        name="outproj",
    )(oT, sgT, woT, xT)


def _alibi_slopes(n):
    return jnp.asarray(2.0 ** (-8.0 * np.arange(1, n + 1) / n), dtype=F32)


def _rope_tables():
    t = np.arange(SEQ)
    axis_dim = HEAD_DIM // 2
    freqs = (1.0 / (np.float32(ROPE_THETA) ** (np.arange(0, axis_dim, 2, dtype=np.float32)
                                                / np.float32(axis_dim)))).astype(np.float32)
    row = (t // GRID_W).astype(np.float32)
    col = (t % GRID_W).astype(np.float32)
    ang = np.concatenate([freqs[:, None] * row[None, :], freqs[:, None] * col[None, :]], axis=0)
    ang = ang.astype(np.float32).astype(np.float64)
    return jnp.asarray(np.cos(ang), F32), jnp.asarray(np.sin(ang), F32)


def _mixer_a(xT, norm, w_in, q_gain, k_gain, sink, w_out, *, last):
    qT, k_tm, vT, sgT = _inproj(xT, norm, w_in, q_gain, k_gain, nq=16, nkv=4, gate_w=1024)
    hp = jnp.stack([_alibi_slopes(16), sink.astype(F32)]) * LOG2E
    return _banded_mixer(hp, qT, k_tm, vT, sgT, w_out, xT, bands=(_Band(A_WINDOW, 1),),
                         slot_heads=tuple((h,) for h in range(16)), has_sink=True, depth=6,
                         row_major_out=last)


def _mixer_b(xT, norm, w_in, q_gain, k_gain, w_out):
    qT, k_tm, vT, sgT = _inproj(xT, norm, w_in, q_gain, k_gain, nq=16, nkv=4, gate_w=1024,
                                rope_tables=_rope_tables())
    oT = _dense_attn(qT, k_tm, vT)
    return _outproj(oT, sgT, w_out, xT)


def _mixer_c(xT, norm, w_in, q_gain, k_gain, w_out):
    qT, k_tm, vT, sgT = _inproj(xT, norm, w_in, q_gain, k_gain, nq=24, nkv=6, gate_w=512)
    slopes = _alibi_slopes(24) * LOG2E
    hp = jnp.stack([slopes, jnp.zeros_like(slopes)])
    bands = tuple(_Band(window // 2, dil) for window, dil in C_GROUPS)
    n_slots = HEADS_PER_PAIR
    slot_heads = tuple(tuple(g * n_slots + s for g in range(len(bands))) for s in range(n_slots))
    return _banded_mixer(hp, qT, k_tm, vT, sgT, w_out, xT, bands=bands, slot_heads=slot_heads,
                         has_sink=False, depth=4)


def kernel(x, l0_norm, l0_w_in, l0_q_gain, l0_k_gain, l0_sink, l0_w_out,
           l1_norm, l1_w_in, l1_q_gain, l1_k_gain, l1_w_out,
           l2_norm, l2_w_in, l2_q_gain, l2_k_gain, l2_w_out,
           l3_norm, l3_w_in, l3_q_gain, l3_k_gain, l3_sink, l3_w_out):
    xT = _mixer_a(x.reshape(SEQ, D_MODEL), l0_norm, l0_w_in, l0_q_gain, l0_k_gain, l0_sink,
                  l0_w_out, last=False)
    xT = _mixer_b(xT, l1_norm, l1_w_in, l1_q_gain, l1_k_gain, l1_w_out)
    xT = _mixer_c(xT, l2_norm, l2_w_in, l2_q_gain, l2_k_gain, l2_w_out)
    out = _mixer_a(xT, l3_norm, l3_w_in, l3_q_gain, l3_k_gain, l3_sink, l3_w_out, last=True)
    return out.reshape(x.shape)
```

```python
import functools
from typing import NamedTuple

import numpy as np
import jax
import jax.numpy as jnp
from jax import lax
from jax.experimental import pallas as pl
from jax.experimental.pallas import tpu as pltpu

D_MODEL = 1024
SEQ = 16384
HEAD_DIM = 64
NORM_EPS = 1e-6
GRID_W = 64
ROPE_THETA = 10000.0
A_WINDOW = 128
C_GROUPS = ((128, 1), (512, 4), (2048, 16))
LOG2E = float(np.log2(np.e))
Q_SCALE = HEAD_DIM ** -0.5 * LOG2E

LANES = 128
BF16_SUBLANES = 16
Q_PER_KV = 4
HEADS_PER_PAIR = 2 * Q_PER_KV
PAIR_ROWS = HEADS_PER_PAIR * HEAD_DIM
V_AUG_ROWS = HEAD_DIM + BF16_SUBLANES
EXP_LIMIT = 64.0
SAFE_DEPTH = 2
VMEM_LIMIT = 56 * 1024 * 1024

BF16 = jnp.bfloat16
F32 = jnp.float32


def _params(*sem):
    return pltpu.CompilerParams(dimension_semantics=sem, vmem_limit_bytes=VMEM_LIMIT)


def _tile_lanes(x, rep):
    return x if rep == 1 else jnp.concatenate([x] * rep, axis=1)


def _resident(shape):
    return pl.BlockSpec(shape, lambda *_: (0,) * len(shape), pipeline_mode=pl.Buffered(1))


def _x_block(x, tm):
    token_major = x.shape == (SEQ, D_MODEL)
    if token_major:
        return pl.BlockSpec((tm, D_MODEL), lambda i: (i, 0)), True
    return pl.BlockSpec((D_MODEL, tm), lambda i: (0, i)), False


def _inproj_kernel(*refs, nq, nkv, gate_w, rope, x_token_major, tm):
    if rope:
        (x_ref, ng_ref, w_ref, qg_ref, kg_ref, cos_ref, sin_ref,
         qT_ref, k_ref, vT_ref, sgT_ref) = refs
        cos, sin = cos_ref[...], sin_ref[...]
    else:
        x_ref, ng_ref, w_ref, qg_ref, kg_ref, qT_ref, k_ref, vT_ref, sgT_ref = refs
    rep = tm // LANES
    x = x_ref[...].T if x_token_major else x_ref[...]
    r = lax.rsqrt(jnp.mean(x * x, axis=0, keepdims=True) + NORM_EPS)
    h = (x * r * _tile_lanes(ng_ref[...], rep)).astype(BF16)
    qg = _tile_lanes(qg_ref[...], rep)
    kg = _tile_lanes(kg_ref[...], rep)

    def head_norm(ph, gain, scale):
        ss = jnp.sum(ph * ph, axis=0, keepdims=True)
        y = ph * (lax.rsqrt(ss * (1.0 / HEAD_DIM) + NORM_EPS) * scale) * gain
        if rope:
            half = HEAD_DIM // 2
            x1, x2 = y[:half], y[half:]
            y = jnp.concatenate([x1 * cos - x2 * sin, x1 * sin + x2 * cos], axis=0)
        return y

    qw, kw = nq * HEAD_DIM, nkv * HEAD_DIM
    chunk = 256
    for c0 in range(0, qw, chunk):
        pc = jnp.dot(w_ref[c0:c0 + chunk, :], h, preferred_element_type=F32)
        for j in range(chunk // HEAD_DIM):
            y = head_norm(pc[j * HEAD_DIM:(j + 1) * HEAD_DIM], qg, Q_SCALE)
            qT_ref[c0 + j * HEAD_DIM:c0 + (j + 1) * HEAD_DIM, :] = y.astype(BF16)
    pk = jnp.dot(w_ref[qw:qw + kw, :], h, preferred_element_type=F32)
    kn = jnp.concatenate(
        [head_norm(pk[j * HEAD_DIM:(j + 1) * HEAD_DIM], kg, 1.0) for j in range(nkv)], axis=0)
    k_ref[...] = kn.T.astype(BF16)
    pv = jnp.dot(w_ref[qw + kw:qw + 2 * kw, :], h, preferred_element_type=F32)
    vT_ref[...] = pv.astype(BF16)
    g0 = qw + 2 * kw
    for c0 in range(0, gate_w, chunk):
        pg = jnp.dot(w_ref[g0 + c0:g0 + c0 + chunk, :], h, preferred_element_type=F32)
        sgT_ref[c0:c0 + chunk, :] = (pg * (1.0 / (1.0 + jnp.exp(-pg)))).astype(BF16)


def _lane_bcast(v):
    return jnp.broadcast_to(v.astype(F32)[:, None], (v.shape[0], LANES))


def _inproj(xT, norm_gain, w_in, q_gain, k_gain, *, nq, nkv, gate_w, rope_tables=None, tm=512):
    qw, kw = nq * HEAD_DIM, nkv * HEAD_DIM
    in_w = qw + 2 * kw + gate_w
    wT = w_in.T
    rope = rope_tables is not None
    if rope:
        perm = np.concatenate([np.arange(0, HEAD_DIM, 2), np.arange(1, HEAD_DIM, 2)])
        rows = np.arange(in_w)
        nqk = nq + nkv
        rows[:nqk * HEAD_DIM] = (np.arange(nqk)[:, None] * HEAD_DIM + perm[None, :]).reshape(-1)
        wT = wT[rows]
        q_gain, k_gain = q_gain[perm], k_gain[perm]
    wT = wT.astype(BF16)
    const = lambda i: (0, 0)
    col = lambda i: (0, i)
    x_spec, x_token_major = _x_block(xT, tm)
    in_specs = [
        x_spec,
        pl.BlockSpec((D_MODEL, LANES), const),
        pl.BlockSpec((in_w, D_MODEL), const),
        pl.BlockSpec((HEAD_DIM, LANES), const),
        pl.BlockSpec((HEAD_DIM, LANES), const),
    ]
    args = [xT, _lane_bcast(norm_gain), wT, _lane_bcast(q_gain), _lane_bcast(k_gain)]
    if rope:
        in_specs += [pl.BlockSpec((HEAD_DIM // 2, tm), col)] * 2
        args += list(rope_tables)
    return pl.pallas_call(
        functools.partial(_inproj_kernel, nq=nq, nkv=nkv, gate_w=gate_w, rope=rope,
                          x_token_major=x_token_major, tm=tm),
        grid=(SEQ // tm,),
        in_specs=in_specs,
        out_specs=[
            pl.BlockSpec((qw, tm), col),
            pl.BlockSpec((tm, kw), lambda i: (i, 0)),
            pl.BlockSpec((kw, tm), col),
            pl.BlockSpec((gate_w, tm), col),
        ],
        out_shape=[
            jax.ShapeDtypeStruct((qw, SEQ), BF16),
            jax.ShapeDtypeStruct((SEQ, kw), BF16),
            jax.ShapeDtypeStruct((kw, SEQ), BF16),
            jax.ShapeDtypeStruct((gate_w, SEQ), BF16),
        ],
        compiler_params=_params("parallel"),
        name="inproj",
    )(*args)


def _fill_qpad(qT_ref, qpad_ref, head0, n_heads, tq):
    zeros = jnp.zeros((HEAD_DIM, tq), BF16)
    for h in range(n_heads):
        q = qT_ref[h * HEAD_DIM:(h + 1) * HEAD_DIM, :]
        lo, hi = (q, zeros) if ((head0 + h) // Q_PER_KV) % 2 == 0 else (zeros, q)
        qpad_ref[h, :HEAD_DIM, :] = lo
        qpad_ref[h, HEAD_DIM:, :] = hi


def _ones_rows(n):
    row = lax.broadcasted_iota(jnp.int32, (BF16_SUBLANES, n), 0)
    return jnp.where(row == 0, 1.0, 0.0).astype(BF16)


def _v_aug(vT_ref, kv, cols, ones_rows):
    return jnp.concatenate([vT_ref[kv * HEAD_DIM:(kv + 1) * HEAD_DIM, cols], ones_rows], axis=0)


def _two_stage_chunks(n_chunks, produce, consume, carry, unroll=2):
    heads = range(HEADS_PER_PAIR)
    assert unroll % 2 == 0 and n_chunks % 2 == 0
    n_tail = 2 + (n_chunks - 2) % unroll

    def stage_at(c, buf, maxes, carry, last=False):
        next_maxes, out = [], []
        for hh in heads:
            if not last:
                next_maxes.append(produce(c + 1, 1 - buf, hh))
            out.append(consume(c, buf, hh, maxes[hh], carry[hh]))
        return tuple(next_maxes), tuple(out)

    def body(jj, state):
        for u in range(unroll):
            state = stage_at(unroll * jj + u, u % 2, *state)
        return state

    state = (tuple(produce(0, 0, hh) for hh in heads), carry)
    state = lax.fori_loop(0, (n_chunks - n_tail) // unroll, body, state)
    for c in range(n_chunks - n_tail, n_chunks):
        state = stage_at(c, c % 2, *state, last=c == n_chunks - 1)
    return state[1]


def _dense_attn_kernel(qT_ref, k_ref, vT_ref, oT_ref, qpad_ref, acc_ref, p0_ref, p1_ref,
                       s0_ref, s1_ref, *, tq, tk, unroll):
    n_chunks = SEQ // tk
    p_bufs, s_bufs = (p0_ref, p1_ref), (s0_ref, s1_ref)
    _fill_qpad(qT_ref, qpad_ref, 0, HEADS_PER_PAIR, tq)
    ones_rows = _ones_rows(tk)

    def chunk_scores(c, hh):
        off = pl.multiple_of(c * tk, tk)
        return jnp.dot(k_ref[pl.ds(off, tk), :], qpad_ref[hh], preferred_element_type=F32)

    def chunk_values(c, hh, p):
        off = pl.multiple_of(c * tk, tk)
        return jnp.dot(_v_aug(vT_ref, hh // Q_PER_KV, pl.ds(off, tk), ones_rows), p,
                       preferred_element_type=F32)

    m_ref = tuple(
        jnp.max(jnp.dot(k_ref[0:LANES, :], qpad_ref[hh], preferred_element_type=F32),
                axis=0, keepdims=True) for hh in range(HEADS_PER_PAIR))
    acc_ref[...] = jnp.zeros(acc_ref.shape, F32)

    def fast_produce(c, buf, hh):
        s = chunk_scores(c, hh)
        p_bufs[buf][hh] = jnp.exp2(s - m_ref[hh]).astype(BF16)
        return jnp.max(s, axis=0, keepdims=True)

    def fast_consume(c, buf, hh, chunk_max, top):
        acc_ref[hh] = acc_ref[hh] + chunk_values(c, hh, p_bufs[buf][hh])
        return jnp.maximum(top, chunk_max)

    top = _two_stage_chunks(n_chunks, fast_produce, fast_consume, m_ref, unroll=unroll)
    excess = functools.reduce(jnp.maximum, [t - r for t, r in zip(top, m_ref)])

    @pl.when(jnp.max(excess) > EXP_LIMIT)
    def _():
        acc_ref[...] = jnp.zeros(acc_ref.shape, F32)

        def safe_produce(c, buf, hh):
            s = chunk_scores(c, hh)
            s_bufs[buf][hh] = s
            return jnp.max(s, axis=0, keepdims=True)

        def safe_consume(c, buf, hh, chunk_max, m):
            mn = jnp.maximum(m, chunk_max)
            p = jnp.exp2(s_bufs[buf][hh] - mn).astype(BF16)
            acc_ref[hh] = jnp.exp2(m - mn) * acc_ref[hh] + chunk_values(c, hh, p)
            return mn

        neg_inf = tuple(jnp.full((1, tq), -jnp.inf, F32) for _ in range(HEADS_PER_PAIR))
        _two_stage_chunks(n_chunks, safe_produce, safe_consume, neg_inf)

    for hh in range(HEADS_PER_PAIR):
        inv = 1.0 / acc_ref[hh, HEAD_DIM:HEAD_DIM + 1, :]
        oT_ref[hh * HEAD_DIM:(hh + 1) * HEAD_DIM, :] = (acc_ref[hh, :HEAD_DIM, :] * inv).astype(BF16)


def _dense_attn(qT, k_tm, vT, *, tq=256, tk=512, unroll=10):
    n_pairs = k_tm.shape[1] // LANES
    p_scratch = pltpu.VMEM((HEADS_PER_PAIR, tk, tq), BF16)
    s_scratch = pltpu.VMEM((HEADS_PER_PAIR, tk, tq), F32)
    return pl.pallas_call(
        functools.partial(_dense_attn_kernel, tq=tq, tk=tk, unroll=unroll),
        grid=(n_pairs, SEQ // tq),
        in_specs=[
            pl.BlockSpec((PAIR_ROWS, tq), lambda p, i: (p, i)),
            pl.BlockSpec((SEQ, LANES), lambda p, i: (0, p), pipeline_mode=pl.Buffered(1)),
            pl.BlockSpec((LANES, SEQ), lambda p, i: (p, 0), pipeline_mode=pl.Buffered(1)),
        ],
        out_specs=pl.BlockSpec((PAIR_ROWS, tq), lambda p, i: (p, i)),
        out_shape=jax.ShapeDtypeStruct(qT.shape, BF16),
        scratch_shapes=[
            pltpu.VMEM((HEADS_PER_PAIR, LANES, tq), BF16),
            pltpu.VMEM((HEADS_PER_PAIR, V_AUG_ROWS, tq), F32),
            p_scratch,
            p_scratch,
            s_scratch,
            s_scratch,
        ],
        compiler_params=_params("parallel", "arbitrary"),
        name="dense_attn",
    )(qT, k_tm, vT)


class _Band(NamedTuple):
    half_width: int
    dil: int

    @property
    def halo(self):
        return -(-self.half_width // LANES) * LANES


def _write_neg_dist(nd_ref, offset, band, lw, tq, hp_ref=None, bias_ref=None):
    rel = (offset + lax.broadcasted_iota(jnp.int32, (lw, tq), 0)
           - lax.broadcasted_iota(jnp.int32, (lw, tq), 1))
    arel = jnp.abs(rel)
    nd = jnp.where(arel <= band.half_width, -arel.astype(F32), -jnp.inf)
    if band.dil > 1:
        nd = jnp.where((rel & (band.dil - 1)) == 0, nd, -jnp.inf)
    nd_ref[...] = nd
    if bias_ref is not None:
        for head in range(bias_ref.shape[0]):
            bias_ref[head] = hp_ref[0, head] * nd - hp_ref[1, head]


def _banded_attn_kernel(hp_ref, qT_ref, k_ref, vT_ref, sgT_ref, w_ref, x_ref, out_ref,
                        qpad_ref, oT_ref, *scratch,
                        tq, bands, slot_heads, has_sink, depth, x_token_major, row_major_out):
    i = pl.program_id(0)
    n_heads = qT_ref.shape[0] // HEAD_DIM
    n_slots = len(slot_heads)
    _fill_qpad(qT_ref, qpad_ref, 0, n_heads, tq)
    if has_sink:
        bias_ref = scratch[3 * len(bands)]
    lws, wstarts, nd_refs, s_bufs, p_bufs, ones = [], [], [], [], [], []
    for b, band in enumerate(bands):
        nd_ref, s_ref, p_ref = scratch[3 * b:3 * b + 3]
        p_bufs.append(p_ref)
        lw = tq + 2 * band.halo
        wstart = jnp.clip(i * tq - band.halo, 0, SEQ - lw)
        offset = wstart - i * tq
        prev_offset = jnp.clip((i - 1) * tq - band.halo, 0, SEQ - lw) - (i - 1) * tq
        pl.when((i == 0) | (offset != prev_offset))(functools.partial(
            _write_neg_dist, nd_ref, offset, band, lw, tq,
            *((hp_ref, bias_ref) if has_sink else ())))
        lws.append(lw)
        wstarts.append(pl.multiple_of(wstart, LANES))
        nd_refs.append(nd_ref)
        s_bufs.append(s_ref)
        ones.append(_ones_rows(lw))

    def scores(slot):
        m = None
        for b in range(len(bands)):
            head = slot_heads[slot][b]
            pair = head // HEADS_PER_PAIR
            s = jnp.dot(k_ref[pl.ds(wstarts[b], lws[b]), pair * LANES:(pair + 1) * LANES],
                        qpad_ref[head], preferred_element_type=F32)
            s = s + hp_ref[0, head] * nd_refs[b][...]
            s_bufs[b][slot % SAFE_DEPTH] = s
            mb = jnp.max(s, axis=0, keepdims=True)
            m = mb if m is None else jnp.maximum(m, mb)
        return m

    def finish(slot, m):
        if has_sink:
            sink = hp_ref[1, slot_heads[slot][0]]
            m = jnp.maximum(m, sink)
        acc = None
        for b in range(len(bands)):
            kv = slot_heads[slot][b] // Q_PER_KV
            p = jnp.exp2(s_bufs[b][slot % SAFE_DEPTH] - m).astype(BF16)
            part = jnp.dot(_v_aug(vT_ref, kv, pl.ds(wstarts[b], lws[b]), ones[b]), p,
                           preferred_element_type=F32)
            acc = part if acc is None else acc + part
        den = acc[HEAD_DIM:HEAD_DIM + 1, :]
        if has_sink:
            den = den + jnp.exp2(sink - m)
        oT_ref[slot * HEAD_DIM:(slot + 1) * HEAD_DIM, :] = (
            acc[:HEAD_DIM, :] * (1.0 / den)).astype(BF16)

    def pipelined(produce, consume, ahead):
        produced = [produce(slot) for slot in range(ahead)]
        for slot in range(n_slots):
            if slot + ahead < n_slots:
                produced.append(produce(slot + ahead))
            consume(slot, produced[slot])
        return produced

    def fast_scores(slot):
        top = None
        for b in range(len(bands)):
            head = slot_heads[slot][b]
            pair = head // HEADS_PER_PAIR
            bias = bias_ref[head] if has_sink else hp_ref[0, head] * nd_refs[b][...]
            t = jnp.dot(k_ref[pl.ds(wstarts[b], lws[b]), pair * LANES:(pair + 1) * LANES],
                        qpad_ref[head], preferred_element_type=F32) + bias
            p_bufs[b][slot % depth] = jnp.exp2(t).astype(BF16)
            tb = jnp.max(t, axis=0, keepdims=True)
            top = tb if top is None else jnp.maximum(top, tb)
        return top

    def fast_finish(slot, _):
        acc = None
        for b in range(len(bands)):
            kv = slot_heads[slot][b] // Q_PER_KV
            part = jnp.dot(_v_aug(vT_ref, kv, pl.ds(wstarts[b], lws[b]), ones[b]),
                           p_bufs[b][slot % depth], preferred_element_type=F32)
            acc = part if acc is None else acc + part
        den = acc[HEAD_DIM:HEAD_DIM + 1, :]
        if has_sink:
            den = den + 1.0
        oT_ref[slot * HEAD_DIM:(slot + 1) * HEAD_DIM, :] = (
            acc[:HEAD_DIM, :] * (1.0 / den)).astype(BF16)

    tops = pipelined(fast_scores, fast_finish, depth - 1)
    unsafe = jnp.max(functools.reduce(jnp.maximum, tops)) > EXP_LIMIT
    if not has_sink:
        unsafe |= jnp.min(functools.reduce(jnp.minimum, tops)) < -EXP_LIMIT

    @pl.when(unsafe)
    def _():
        pipelined(scores, finish, SAFE_DEPTH - 1)

    og = oT_ref[...] * sgT_ref[...]
    x = x_ref[...].T if x_token_major else x_ref[...]
    y = x + jnp.dot(w_ref[...], og, preferred_element_type=F32)
    out_ref[...] = y.T if row_major_out else y


def _residual_out(row_major_out, tm):
    if row_major_out:
        return (pl.BlockSpec((tm, D_MODEL), lambda i: (i, 0)),
                jax.ShapeDtypeStruct((SEQ, D_MODEL), F32))
    return (pl.BlockSpec((D_MODEL, tm), lambda i: (0, i)),
            jax.ShapeDtypeStruct((D_MODEL, SEQ), F32))


def _banded_mixer(head_params, qT, k_tm, vT, sgT, w_out, xT, *, bands, slot_heads, has_sink,
                  depth, row_major_out=False, tq=256):
    n_heads = qT.shape[0] // HEAD_DIM
    width = len(slot_heads) * HEAD_DIM
    col = lambda i: (0, i)
    scratch = [pltpu.VMEM((n_heads, LANES, tq), BF16), pltpu.VMEM((width, tq), BF16)]
    for band in bands:
        lw = tq + 2 * band.halo
        scratch += [pltpu.VMEM((lw, tq), F32), pltpu.VMEM((SAFE_DEPTH, lw, tq), F32),
                    pltpu.VMEM((depth, lw, tq), BF16)]
    if has_sink:
        assert len(bands) == 1, "the per-head bias cache is sized for one band"
        scratch += [pltpu.VMEM((n_heads, lw, tq), F32)]
    x_spec, x_token_major = _x_block(xT, tq)
    out_spec, out_shape = _residual_out(row_major_out, tq)
    woT = w_out.T.astype(BF16)
    return pl.pallas_call(
        functools.partial(_banded_attn_kernel, tq=tq, bands=bands, slot_heads=slot_heads,
                          has_sink=has_sink, depth=depth, x_token_major=x_token_major,
                          row_major_out=row_major_out),
        grid=(SEQ // tq,),
        in_specs=[
            pl.BlockSpec(memory_space=pltpu.SMEM),
            pl.BlockSpec((qT.shape[0], tq), col),
            _resident(k_tm.shape),
            _resident(vT.shape),
            pl.BlockSpec((width, tq), col),
            _resident(woT.shape),
            x_spec,
        ],
        out_specs=out_spec,
        out_shape=out_shape,
        scratch_shapes=scratch,
        compiler_params=_params("arbitrary"),
        name="banded_mixer",
    )(head_params, qT, k_tm, vT, sgT, woT, xT)


def _outproj_kernel(oT_ref, sgT_ref, w_ref, x_ref, out_ref, *, x_token_major, row_major_out):
    og = oT_ref[...] * sgT_ref[...]
    x = x_ref[...].T if x_token_major else x_ref[...]
    y = x + jnp.dot(w_ref[...], og, preferred_element_type=F32)
    out_ref[...] = y.T if row_major_out else y


def _outproj(oT, sgT, w_out, xT, *, row_major_out=False, tm=512):
    width = sgT.shape[0]
    woT = w_out.T.astype(BF16)
    col = lambda i: (0, i)
    x_spec, x_token_major = _x_block(xT, tm)
    out_spec, out_shape = _residual_out(row_major_out, tm)
    return pl.pallas_call(
        functools.partial(_outproj_kernel, x_token_major=x_token_major,
                          row_major_out=row_major_out),
        grid=(SEQ // tm,),
        in_specs=[
            pl.BlockSpec((width, tm), col),
            pl.BlockSpec((width, tm), col),
            pl.BlockSpec((D_MODEL, width), lambda i: (0, 0)),
            x_spec,
        ],
        out_specs=out_spec,
        out_shape=out_shape,
        compiler_params=_params("parallel"),
        name="outproj",
    )(oT, sgT, woT, xT)


def _alibi_slopes(n):
    return jnp.asarray(2.0 ** (-8.0 * np.arange(1, n + 1) / n), dtype=F32)


def _rope_tables():
    t = np.arange(SEQ)
    axis_dim = HEAD_DIM // 2
    freqs = (1.0 / (np.float32(ROPE_THETA) ** (np.arange(0, axis_dim, 2, dtype=np.float32)
                                                / np.float32(axis_dim)))).astype(np.float32)
    row = (t // GRID_W).astype(np.float32)
    col = (t % GRID_W).astype(np.float32)
    ang = np.concatenate([freqs[:, None] * row[None, :], freqs[:, None] * col[None, :]], axis=0)
    ang = ang.astype(np.float32).astype(np.float64)
    return jnp.asarray(np.cos(ang), F32), jnp.asarray(np.sin(ang), F32)


def _mixer_a(xT, norm, w_in, q_gain, k_gain, sink, w_out, *, last):
    qT, k_tm, vT, sgT = _inproj(xT, norm, w_in, q_gain, k_gain, nq=16, nkv=4, gate_w=1024)
    hp = jnp.stack([_alibi_slopes(16), sink.astype(F32)]) * LOG2E
    return _banded_mixer(hp, qT, k_tm, vT, sgT, w_out, xT, bands=(_Band(A_WINDOW, 1),),
                         slot_heads=tuple((h,) for h in range(16)), has_sink=True, depth=6,
                         row_major_out=last)


def _mixer_b(xT, norm, w_in, q_gain, k_gain, w_out):
    qT, k_tm, vT, sgT = _inproj(xT, norm, w_in, q_gain, k_gain, nq=16, nkv=4, gate_w=1024,
                                rope_tables=_rope_tables())
    oT = _dense_attn(qT, k_tm, vT)
    return _outproj(oT, sgT, w_out, xT)


def _mixer_c(xT, norm, w_in, q_gain, k_gain, w_out):
    qT, k_tm, vT, sgT = _inproj(xT, norm, w_in, q_gain, k_gain, nq=24, nkv=6, gate_w=512)
    slopes = _alibi_slopes(24) * LOG2E
    hp = jnp.stack([slopes, jnp.zeros_like(slopes)])
    bands = tuple(_Band(window // 2, dil) for window, dil in C_GROUPS)
    n_slots = HEADS_PER_PAIR
    slot_heads = tuple(tuple(g * n_slots + s for g in range(len(bands))) for s in range(n_slots))
    return _banded_mixer(hp, qT, k_tm, vT, sgT, w_out, xT, bands=bands, slot_heads=slot_heads,
                         has_sink=False, depth=4)


def kernel(x, l0_norm, l0_w_in, l0_q_gain, l0_k_gain, l0_sink, l0_w_out,
           l1_norm, l1_w_in, l1_q_gain, l1_k_gain, l1_w_out,
           l2_norm, l2_w_in, l2_q_gain, l2_k_gain, l2_w_out,
           l3_norm, l3_w_in, l3_q_gain, l3_k_gain, l3_sink, l3_w_out):
    xT = _mixer_a(x.reshape(SEQ, D_MODEL), l0_norm, l0_w_in, l0_q_gain, l0_k_gain, l0_sink,
                  l0_w_out, last=False)
    xT = _mixer_b(xT, l1_norm, l1_w_in, l1_q_gain, l1_k_gain, l1_w_out)
    xT = _mixer_c(xT, l2_norm, l2_w_in, l2_q_gain, l2_k_gain, l2_w_out)
    out = _mixer_a(xT, l3_norm, l3_w_in, l3_q_gain, l3_k_gain, l3_sink, l3_w_out, last=True)
    return out.reshape(x.shape)
```

```python
import functools
from typing import NamedTuple

import numpy as np
import jax
import jax.numpy as jnp
from jax import lax
from jax.experimental import pallas as pl
from jax.experimental.pallas import tpu as pltpu

D_MODEL = 1024
SEQ = 16384
HEAD_DIM = 64
NORM_EPS = 1e-6
GRID_W = 64
ROPE_THETA = 10000.0
A_WINDOW = 128
C_GROUPS = ((128, 1), (512, 4), (2048, 16))
LOG2E = float(np.log2(np.e))
Q_SCALE = HEAD_DIM ** -0.5 * LOG2E

LANES = 128
BF16_SUBLANES = 16
Q_PER_KV = 4
HEADS_PER_PAIR = 2 * Q_PER_KV
PAIR_ROWS = HEADS_PER_PAIR * HEAD_DIM
V_AUG_ROWS = HEAD_DIM + BF16_SUBLANES
EXP_LIMIT = 64.0
SAFE_DEPTH = 2
VMEM_LIMIT = 56 * 1024 * 1024

BF16 = jnp.bfloat16
F32 = jnp.float32


def _params(*sem):
    return pltpu.CompilerParams(dimension_semantics=sem, vmem_limit_bytes=VMEM_LIMIT)


def _tile_lanes(x, rep):
    return x if rep == 1 else jnp.concatenate([x] * rep, axis=1)


def _resident(shape):
    return pl.BlockSpec(shape, lambda *_: (0,) * len(shape), pipeline_mode=pl.Buffered(1))


def _x_block(x, tm):
    token_major = x.shape == (SEQ, D_MODEL)
    if token_major:
        return pl.BlockSpec((tm, D_MODEL), lambda i: (i, 0)), True
    return pl.BlockSpec((D_MODEL, tm), lambda i: (0, i)), False


def _inproj_kernel(*refs, nq, nkv, gate_w, rope, x_token_major, tm):
    if rope:
        (x_ref, ng_ref, w_ref, qg_ref, kg_ref, cos_ref, sin_ref,
         qT_ref, k_ref, vT_ref, sgT_ref) = refs
        cos, sin = cos_ref[...], sin_ref[...]
    else:
        x_ref, ng_ref, w_ref, qg_ref, kg_ref, qT_ref, k_ref, vT_ref, sgT_ref = refs
    rep = tm // LANES
    x = x_ref[...].T if x_token_major else x_ref[...]
    r = lax.rsqrt(jnp.mean(x * x, axis=0, keepdims=True) + NORM_EPS)
    h = (x * r * _tile_lanes(ng_ref[...], rep)).astype(BF16)
    qg = _tile_lanes(qg_ref[...], rep)
    kg = _tile_lanes(kg_ref[...], rep)

    def head_norm(ph, gain, scale):
        ss = jnp.sum(ph * ph, axis=0, keepdims=True)
        y = ph * (lax.rsqrt(ss * (1.0 / HEAD_DIM) + NORM_EPS) * scale) * gain
        if rope:
            half = HEAD_DIM // 2
            x1, x2 = y[:half], y[half:]
            y = jnp.concatenate([x1 * cos - x2 * sin, x1 * sin + x2 * cos], axis=0)
        return y

    qw, kw = nq * HEAD_DIM, nkv * HEAD_DIM
    chunk = 256
    for c0 in range(0, qw, chunk):
        pc = jnp.dot(w_ref[c0:c0 + chunk, :], h, preferred_element_type=F32)
        for j in range(chunk // HEAD_DIM):
            y = head_norm(pc[j * HEAD_DIM:(j + 1) * HEAD_DIM], qg, Q_SCALE)
            qT_ref[c0 + j * HEAD_DIM:c0 + (j + 1) * HEAD_DIM, :] = y.astype(BF16)
    pk = jnp.dot(w_ref[qw:qw + kw, :], h, preferred_element_type=F32)
    kn = jnp.concatenate(
        [head_norm(pk[j * HEAD_DIM:(j + 1) * HEAD_DIM], kg, 1.0) for j in range(nkv)], axis=0)
    k_ref[...] = kn.T.astype(BF16)
    pv = jnp.dot(w_ref[qw + kw:qw + 2 * kw, :], h, preferred_element_type=F32)
    vT_ref[...] = pv.astype(BF16)
    g0 = qw + 2 * kw
    for c0 in range(0, gate_w, chunk):
        pg = jnp.dot(w_ref[g0 + c0:g0 + c0 + chunk, :], h, preferred_element_type=F32)
        sgT_ref[c0:c0 + chunk, :] = (pg * (1.0 / (1.0 + jnp.exp(-pg)))).astype(BF16)


def _lane_bcast(v):
    return jnp.broadcast_to(v.astype(F32)[:, None], (v.shape[0], LANES))


def _inproj(xT, norm_gain, w_in, q_gain, k_gain, *, nq, nkv, gate_w, rope_tables=None, tm=512):
    qw, kw = nq * HEAD_DIM, nkv * HEAD_DIM
    in_w = qw + 2 * kw + gate_w
    wT = w_in.T
    rope = rope_tables is not None
    if rope:
        perm = np.concatenate([np.arange(0, HEAD_DIM, 2), np.arange(1, HEAD_DIM, 2)])
        rows = np.arange(in_w)
        nqk = nq + nkv
        rows[:nqk * HEAD_DIM] = (np.arange(nqk)[:, None] * HEAD_DIM + perm[None, :]).reshape(-1)
        wT = wT[rows]
        q_gain, k_gain = q_gain[perm], k_gain[perm]
    wT = wT.astype(BF16)
    const = lambda i: (0, 0)
    col = lambda i: (0, i)
    x_spec, x_token_major = _x_block(xT, tm)
    in_specs = [
        x_spec,
        pl.BlockSpec((D_MODEL, LANES), const),
        pl.BlockSpec((in_w, D_MODEL), const),
        pl.BlockSpec((HEAD_DIM, LANES), const),
        pl.BlockSpec((HEAD_DIM, LANES), const),
    ]
    args = [xT, _lane_bcast(norm_gain), wT, _lane_bcast(q_gain), _lane_bcast(k_gain)]
    if rope:
        in_specs += [pl.BlockSpec((HEAD_DIM // 2, tm), col)] * 2
        args += list(rope_tables)
    return pl.pallas_call(
        functools.partial(_inproj_kernel, nq=nq, nkv=nkv, gate_w=gate_w, rope=rope,
                          x_token_major=x_token_major, tm=tm),
        grid=(SEQ // tm,),
        in_specs=in_specs,
        out_specs=[
            pl.BlockSpec((qw, tm), col),
            pl.BlockSpec((tm, kw), lambda i: (i, 0)),
            pl.BlockSpec((kw, tm), col),
            pl.BlockSpec((gate_w, tm), col),
        ],
        out_shape=[
            jax.ShapeDtypeStruct((qw, SEQ), BF16),
            jax.ShapeDtypeStruct((SEQ, kw), BF16),
            jax.ShapeDtypeStruct((kw, SEQ), BF16),
            jax.ShapeDtypeStruct((gate_w, SEQ), BF16),
        ],
        compiler_params=_params("parallel"),
        name="inproj",
    )(*args)


def _fill_qpad(qT_ref, qpad_ref, head0, n_heads, tq):
    zeros = jnp.zeros((HEAD_DIM, tq), BF16)
    for h in range(n_heads):
        q = qT_ref[h * HEAD_DIM:(h + 1) * HEAD_DIM, :]
        lo, hi = (q, zeros) if ((head0 + h) // Q_PER_KV) % 2 == 0 else (zeros, q)
        qpad_ref[h, :HEAD_DIM, :] = lo
        qpad_ref[h, HEAD_DIM:, :] = hi


def _ones_rows(n):
    row = lax.broadcasted_iota(jnp.int32, (BF16_SUBLANES, n), 0)
    return jnp.where(row == 0, 1.0, 0.0).astype(BF16)


def _v_aug(vT_ref, kv, cols, ones_rows):
    return jnp.concatenate([vT_ref[kv * HEAD_DIM:(kv + 1) * HEAD_DIM, cols], ones_rows], axis=0)


def _two_stage_chunks(n_chunks, produce, consume, carry, unroll=2):
    heads = range(HEADS_PER_PAIR)
    assert unroll % 2 == 0 and n_chunks % 2 == 0
    n_tail = 2 + (n_chunks - 2) % unroll

    def stage_at(c, buf, maxes, carry, last=False):
        next_maxes, out = [], []
        for hh in heads:
            if not last:
                next_maxes.append(produce(c + 1, 1 - buf, hh))
            out.append(consume(c, buf, hh, maxes[hh], carry[hh]))
        return tuple(next_maxes), tuple(out)

    def body(jj, state):
        for u in range(unroll):
            state = stage_at(unroll * jj + u, u % 2, *state)
        return state

    state = (tuple(produce(0, 0, hh) for hh in heads), carry)
    state = lax.fori_loop(0, (n_chunks - n_tail) // unroll, body, state)
    for c in range(n_chunks - n_tail, n_chunks):
        state = stage_at(c, c % 2, *state, last=c == n_chunks - 1)
    return state[1]


def _dense_attn_kernel(qT_ref, k_ref, vT_ref, oT_ref, qpad_ref, acc_ref, p0_ref, p1_ref,
                       s0_ref, s1_ref, *, tq, tk, unroll):
    n_chunks = SEQ // tk
    p_bufs, s_bufs = (p0_ref, p1_ref), (s0_ref, s1_ref)
    _fill_qpad(qT_ref, qpad_ref, 0, HEADS_PER_PAIR, tq)
    ones_rows = _ones_rows(tk)

    def chunk_scores(c, hh):
        off = pl.multiple_of(c * tk, tk)
        return jnp.dot(k_ref[pl.ds(off, tk), :], qpad_ref[hh], preferred_element_type=F32)

    def chunk_values(c, hh, p):
        off = pl.multiple_of(c * tk, tk)
        return jnp.dot(_v_aug(vT_ref, hh // Q_PER_KV, pl.ds(off, tk), ones_rows), p,
                       preferred_element_type=F32)

    acc_ref[...] = jnp.zeros(acc_ref.shape, F32)

    def fast_produce(c, buf, hh):
        s = chunk_scores(c, hh)
        p = jnp.exp2(s)
        p_bufs[buf][hh] = p.astype(BF16)
        return jnp.max(s, axis=0, keepdims=True), jnp.sum(p, axis=0, keepdims=True)

    def fast_consume(c, buf, hh, chunk_stats, carry):
        off = pl.multiple_of(c * tk, tk)
        kv = hh // Q_PER_KV
        acc_ref[hh, :HEAD_DIM, :] = acc_ref[hh, :HEAD_DIM, :] + jnp.dot(
            vT_ref[kv * HEAD_DIM:(kv + 1) * HEAD_DIM, pl.ds(off, tk)], p_bufs[buf][hh],
            preferred_element_type=F32)
        return jnp.maximum(carry[0], chunk_stats[0]), carry[1] + chunk_stats[1]

    neg_inf = tuple(jnp.full((1, tq), -jnp.inf, F32) for _ in range(HEADS_PER_PAIR))
    zero = jnp.zeros((1, tq), F32)
    stats = _two_stage_chunks(n_chunks, fast_produce, fast_consume,
                              tuple((m, zero) for m in neg_inf), unroll=unroll)
    for hh in range(HEADS_PER_PAIR):
        acc_ref[hh, HEAD_DIM:HEAD_DIM + 1, :] = stats[hh][1]
    top = [st[0] for st in stats]
    unsafe = ((jnp.max(functools.reduce(jnp.maximum, top)) > EXP_LIMIT)
              | (jnp.min(functools.reduce(jnp.minimum, top)) < -EXP_LIMIT))

    @pl.when(unsafe)
    def _():
        acc_ref[...] = jnp.zeros(acc_ref.shape, F32)

        def safe_produce(c, buf, hh):
            s = chunk_scores(c, hh)
            s_bufs[buf][hh] = s
            return jnp.max(s, axis=0, keepdims=True)

        def safe_consume(c, buf, hh, chunk_max, m):
            mn = jnp.maximum(m, chunk_max)
            p = jnp.exp2(s_bufs[buf][hh] - mn).astype(BF16)
            acc_ref[hh] = jnp.exp2(m - mn) * acc_ref[hh] + chunk_values(c, hh, p)
            return mn

        _two_stage_chunks(n_chunks, safe_produce, safe_consume, neg_inf)

    for hh in range(HEADS_PER_PAIR):
        inv = 1.0 / acc_ref[hh, HEAD_DIM:HEAD_DIM + 1, :]
        oT_ref[hh * HEAD_DIM:(hh + 1) * HEAD_DIM, :] = (acc_ref[hh, :HEAD_DIM, :] * inv).astype(BF16)


def _dense_attn(qT, k_tm, vT, *, tq=256, tk=512, unroll=10):
    n_pairs = k_tm.shape[1] // LANES
    p_scratch = pltpu.VMEM((HEADS_PER_PAIR, tk, tq), BF16)
    s_scratch = pltpu.VMEM((HEADS_PER_PAIR, tk, tq), F32)
    return pl.pallas_call(
        functools.partial(_dense_attn_kernel, tq=tq, tk=tk, unroll=unroll),
        grid=(n_pairs, SEQ // tq),
        in_specs=[
            pl.BlockSpec((PAIR_ROWS, tq), lambda p, i: (p, i)),
            pl.BlockSpec((SEQ, LANES), lambda p, i: (0, p), pipeline_mode=pl.Buffered(1)),
            pl.BlockSpec((LANES, SEQ), lambda p, i: (p, 0), pipeline_mode=pl.Buffered(1)),
        ],
        out_specs=pl.BlockSpec((PAIR_ROWS, tq), lambda p, i: (p, i)),
        out_shape=jax.ShapeDtypeStruct(qT.shape, BF16),
        scratch_shapes=[
            pltpu.VMEM((HEADS_PER_PAIR, LANES, tq), BF16),
            pltpu.VMEM((HEADS_PER_PAIR, V_AUG_ROWS, tq), F32),
            p_scratch,
            p_scratch,
            s_scratch,
            s_scratch,
        ],
        compiler_params=_params("parallel", "arbitrary"),
        name="dense_attn",
    )(qT, k_tm, vT)


class _Band(NamedTuple):
    half_width: int
    dil: int

    @property
    def halo(self):
        return -(-self.half_width // LANES) * LANES


def _write_neg_dist(nd_ref, offset, band, lw, tq, hp_ref=None, bias_ref=None):
    rel = (offset + lax.broadcasted_iota(jnp.int32, (lw, tq), 0)
           - lax.broadcasted_iota(jnp.int32, (lw, tq), 1))
    arel = jnp.abs(rel)
    nd = jnp.where(arel <= band.half_width, -arel.astype(F32), -jnp.inf)
    if band.dil > 1:
        nd = jnp.where((rel & (band.dil - 1)) == 0, nd, -jnp.inf)
    nd_ref[...] = nd
    if bias_ref is not None:
        for head in range(bias_ref.shape[0]):
            bias_ref[head] = hp_ref[0, head] * nd - hp_ref[1, head]


def _banded_attn_kernel(hp_ref, qT_ref, k_ref, vT_ref, sgT_ref, w_ref, x_ref, out_ref,
                        qpad_ref, oT_ref, *scratch,
                        tq, bands, slot_heads, has_sink, depth, x_token_major, row_major_out):
    i = pl.program_id(0)
    n_heads = qT_ref.shape[0] // HEAD_DIM
    n_slots = len(slot_heads)
    _fill_qpad(qT_ref, qpad_ref, 0, n_heads, tq)
    if has_sink:
        bias_ref = scratch[3 * len(bands)]
    lws, wstarts, nd_refs, s_bufs, p_bufs, ones = [], [], [], [], [], []
    for b, band in enumerate(bands):
        nd_ref, s_ref, p_ref = scratch[3 * b:3 * b + 3]
        p_bufs.append(p_ref)
        lw = tq + 2 * band.halo
        wstart = jnp.clip(i * tq - band.halo, 0, SEQ - lw)
        offset = wstart - i * tq
        prev_offset = jnp.clip((i - 1) * tq - band.halo, 0, SEQ - lw) - (i - 1) * tq
        pl.when((i == 0) | (offset != prev_offset))(functools.partial(
            _write_neg_dist, nd_ref, offset, band, lw, tq,
            *((hp_ref, bias_ref) if has_sink else ())))
        lws.append(lw)
        wstarts.append(pl.multiple_of(wstart, LANES))
        nd_refs.append(nd_ref)
        s_bufs.append(s_ref)
        ones.append(_ones_rows(lw))

    def scores(slot):
        m = None
        for b in range(len(bands)):
            head = slot_heads[slot][b]
            pair = head // HEADS_PER_PAIR
            s = jnp.dot(k_ref[pl.ds(wstarts[b], lws[b]), pair * LANES:(pair + 1) * LANES],
                        qpad_ref[head], preferred_element_type=F32)
            s = s + hp_ref[0, head] * nd_refs[b][...]
            s_bufs[b][slot % SAFE_DEPTH] = s
            mb = jnp.max(s, axis=0, keepdims=True)
            m = mb if m is None else jnp.maximum(m, mb)
        return m

    def finish(slot, m):
        if has_sink:
            sink = hp_ref[1, slot_heads[slot][0]]
            m = jnp.maximum(m, sink)
        acc = None
        for b in range(len(bands)):
            kv = slot_heads[slot][b] // Q_PER_KV
            p = jnp.exp2(s_bufs[b][slot % SAFE_DEPTH] - m).astype(BF16)
            part = jnp.dot(_v_aug(vT_ref, kv, pl.ds(wstarts[b], lws[b]), ones[b]), p,
                           preferred_element_type=F32)
            acc = part if acc is None else acc + part
        den = acc[HEAD_DIM:HEAD_DIM + 1, :]
        if has_sink:
            den = den + jnp.exp2(sink - m)
        oT_ref[slot * HEAD_DIM:(slot + 1) * HEAD_DIM, :] = (
            acc[:HEAD_DIM, :] * (1.0 / den)).astype(BF16)

    def pipelined(produce, consume, ahead):
        produced = [produce(slot) for slot in range(ahead)]
        for slot in range(n_slots):
            if slot + ahead < n_slots:
                produced.append(produce(slot + ahead))
            consume(slot, produced[slot])
        return produced

    def fast_scores(slot):
        top = None
        for b in range(len(bands)):
            head = slot_heads[slot][b]
            pair = head // HEADS_PER_PAIR
            bias = bias_ref[head] if has_sink else hp_ref[0, head] * nd_refs[b][...]
            t = jnp.dot(k_ref[pl.ds(wstarts[b], lws[b]), pair * LANES:(pair + 1) * LANES],
                        qpad_ref[head], preferred_element_type=F32) + bias
            p_bufs[b][slot % depth] = jnp.exp2(t).astype(BF16)
            tb = jnp.max(t, axis=0, keepdims=True)
            top = tb if top is None else jnp.maximum(top, tb)
        return top

    def fast_finish(slot, _):
        acc = None
        for b in range(len(bands)):
            kv = slot_heads[slot][b] // Q_PER_KV
            part = jnp.dot(_v_aug(vT_ref, kv, pl.ds(wstarts[b], lws[b]), ones[b]),
                           p_bufs[b][slot % depth], preferred_element_type=F32)
            acc = part if acc is None else acc + part
        den = acc[HEAD_DIM:HEAD_DIM + 1, :]
        if has_sink:
            den = den + 1.0
        oT_ref[slot * HEAD_DIM:(slot + 1) * HEAD_DIM, :] = (
            acc[:HEAD_DIM, :] * (1.0 / den)).astype(BF16)

    tops = pipelined(fast_scores, fast_finish, depth - 1)
    unsafe = jnp.max(functools.reduce(jnp.maximum, tops)) > EXP_LIMIT
    if not has_sink:
        unsafe |= jnp.min(functools.reduce(jnp.minimum, tops)) < -EXP_LIMIT

    @pl.when(unsafe)
    def _():
        pipelined(scores, finish, SAFE_DEPTH - 1)

    og = oT_ref[...] * sgT_ref[...]
    x = x_ref[...].T if x_token_major else x_ref[...]
    y = x + jnp.dot(w_ref[...], og, preferred_element_type=F32)
    out_ref[...] = y.T if row_major_out else y


def _residual_out(row_major_out, tm):
    if row_major_out:
        return (pl.BlockSpec((tm, D_MODEL), lambda i: (i, 0)),
                jax.ShapeDtypeStruct((SEQ, D_MODEL), F32))
    return (pl.BlockSpec((D_MODEL, tm), lambda i: (0, i)),
            jax.ShapeDtypeStruct((D_MODEL, SEQ), F32))


def _banded_mixer(head_params, qT, k_tm, vT, sgT, w_out, xT, *, bands, slot_heads, has_sink,
                  depth, row_major_out=False, tq=256):
    n_heads = qT.shape[0] // HEAD_DIM
    width = len(slot_heads) * HEAD_DIM
    col = lambda i: (0, i)
    scratch = [pltpu.VMEM((n_heads, LANES, tq), BF16), pltpu.VMEM((width, tq), BF16)]
    for band in bands:
        lw = tq + 2 * band.halo
        scratch += [pltpu.VMEM((lw, tq), F32), pltpu.VMEM((SAFE_DEPTH, lw, tq), F32),
                    pltpu.VMEM((depth, lw, tq), BF16)]
    if has_sink:
        assert len(bands) == 1, "the per-head bias cache is sized for one band"
        scratch += [pltpu.VMEM((n_heads, lw, tq), F32)]
    x_spec, x_token_major = _x_block(xT, tq)
    out_spec, out_shape = _residual_out(row_major_out, tq)
    woT = w_out.T.astype(BF16)
    return pl.pallas_call(
        functools.partial(_banded_attn_kernel, tq=tq, bands=bands, slot_heads=slot_heads,
                          has_sink=has_sink, depth=depth, x_token_major=x_token_major,
                          row_major_out=row_major_out),
        grid=(SEQ // tq,),
        in_specs=[
            pl.BlockSpec(memory_space=pltpu.SMEM),
            pl.BlockSpec((qT.shape[0], tq), col),
            _resident(k_tm.shape),
            _resident(vT.shape),
            pl.BlockSpec((width, tq), col),
            _resident(woT.shape),
            x_spec,
        ],
        out_specs=out_spec,
        out_shape=out_shape,
        scratch_shapes=scratch,
        compiler_params=_params("arbitrary"),
        name="banded_mixer",
    )(head_params, qT, k_tm, vT, sgT, woT, xT)


def _outproj_kernel(oT_ref, sgT_ref, w_ref, x_ref, out_ref, *, x_token_major, row_major_out):
    og = oT_ref[...] * sgT_ref[...]
    x = x_ref[...].T if x_token_major else x_ref[...]
    y = x + jnp.dot(w_ref[...], og, preferred_element_type=F32)
    out_ref[...] = y.T if row_major_out else y


def _outproj(oT, sgT, w_out, xT, *, row_major_out=False, tm=512):
    width = sgT.shape[0]
    woT = w_out.T.astype(BF16)
    col = lambda i: (0, i)
    x_spec, x_token_major = _x_block(xT, tm)
    out_spec, out_shape = _residual_out(row_major_out, tm)
    return pl.pallas_call(
        functools.partial(_outproj_kernel, x_token_major=x_token_major,
                          row_major_out=row_major_out),
        grid=(SEQ // tm,),
        in_specs=[
            pl.BlockSpec((width, tm), col),
            pl.BlockSpec((width, tm), col),
            pl.BlockSpec((D_MODEL, width), lambda i: (0, 0)),
            x_spec,
        ],
        out_specs=out_spec,
        out_shape=out_shape,
        compiler_params=_params("parallel"),
        name="outproj",
    )(oT, sgT, woT, xT)


def _alibi_slopes(n):
    return jnp.asarray(2.0 ** (-8.0 * np.arange(1, n + 1) / n), dtype=F32)


def _rope_tables():
    t = np.arange(SEQ)
    axis_dim = HEAD_DIM // 2
    freqs = (1.0 / (np.float32(ROPE_THETA) ** (np.arange(0, axis_dim, 2, dtype=np.float32)
                                                / np.float32(axis_dim)))).astype(np.float32)
    row = (t // GRID_W).astype(np.float32)
    col = (t % GRID_W).astype(np.float32)
    ang = np.concatenate([freqs[:, None] * row[None, :], freqs[:, None] * col[None, :]], axis=0)
    ang = ang.astype(np.float32).astype(np.float64)
    return jnp.asarray(np.cos(ang), F32), jnp.asarray(np.sin(ang), F32)


def _mixer_a(xT, norm, w_in, q_gain, k_gain, sink, w_out, *, last):
    qT, k_tm, vT, sgT = _inproj(xT, norm, w_in, q_gain, k_gain, nq=16, nkv=4, gate_w=1024)
    hp = jnp.stack([_alibi_slopes(16), sink.astype(F32)]) * LOG2E
    return _banded_mixer(hp, qT, k_tm, vT, sgT, w_out, xT, bands=(_Band(A_WINDOW, 1),),
                         slot_heads=tuple((h,) for h in range(16)), has_sink=True, depth=6,
                         row_major_out=last)


def _mixer_b(xT, norm, w_in, q_gain, k_gain, w_out):
    qT, k_tm, vT, sgT = _inproj(xT, norm, w_in, q_gain, k_gain, nq=16, nkv=4, gate_w=1024,
                                rope_tables=_rope_tables())
    oT = _dense_attn(qT, k_tm, vT)
    return _outproj(oT, sgT, w_out, xT)


def _mixer_c(xT, norm, w_in, q_gain, k_gain, w_out):
    qT, k_tm, vT, sgT = _inproj(xT, norm, w_in, q_gain, k_gain, nq=24, nkv=6, gate_w=512)
    slopes = _alibi_slopes(24) * LOG2E
    hp = jnp.stack([slopes, jnp.zeros_like(slopes)])
    bands = tuple(_Band(window // 2, dil) for window, dil in C_GROUPS)
    n_slots = HEADS_PER_PAIR
    slot_heads = tuple(tuple(g * n_slots + s for g in range(len(bands))) for s in range(n_slots))
    return _banded_mixer(hp, qT, k_tm, vT, sgT, w_out, xT, bands=bands, slot_heads=slot_heads,
                         has_sink=False, depth=4)


def kernel(x, l0_norm, l0_w_in, l0_q_gain, l0_k_gain, l0_sink, l0_w_out,
           l1_norm, l1_w_in, l1_q_gain, l1_k_gain, l1_w_out,
           l2_norm, l2_w_in, l2_q_gain, l2_k_gain, l2_w_out,
           l3_norm, l3_w_in, l3_q_gain, l3_k_gain, l3_sink, l3_w_out):
    xT = _mixer_a(x.reshape(SEQ, D_MODEL), l0_norm, l0_w_in, l0_q_gain, l0_k_gain, l0_sink,
                  l0_w_out, last=False)
    xT = _mixer_b(xT, l1_norm, l1_w_in, l1_q_gain, l1_k_gain, l1_w_out)
    xT = _mixer_c(xT, l2_norm, l2_w_in, l2_q_gain, l2_k_gain, l2_w_out)
    out = _mixer_a(xT, l3_norm, l3_w_in, l3_q_gain, l3_k_gain, l3_sink, l3_w_out, last=True)
    return out.reshape(x.shape)
```

```python
import functools
from typing import NamedTuple

import numpy as np
import jax
import jax.numpy as jnp
from jax import lax
from jax.experimental import pallas as pl
from jax.experimental.pallas import tpu as pltpu

D_MODEL = 1024
SEQ = 16384
HEAD_DIM = 64
NORM_EPS = 1e-6
GRID_W = 64
ROPE_THETA = 10000.0
A_WINDOW = 128
C_GROUPS = ((128, 1), (512, 4), (2048, 16))
LOG2E = float(np.log2(np.e))
Q_SCALE = HEAD_DIM ** -0.5 * LOG2E

LANES = 128
BF16_SUBLANES = 16
Q_PER_KV = 4
HEADS_PER_PAIR = 2 * Q_PER_KV
PAIR_ROWS = HEADS_PER_PAIR * HEAD_DIM
V_AUG_ROWS = HEAD_DIM + BF16_SUBLANES
EXP_LIMIT = 64.0
SAFE_DEPTH = 2
VMEM_LIMIT = 56 * 1024 * 1024

BF16 = jnp.bfloat16
F32 = jnp.float32


def _params(*sem):
    return pltpu.CompilerParams(dimension_semantics=sem, vmem_limit_bytes=VMEM_LIMIT)


def _tile_lanes(x, rep):
    return x if rep == 1 else jnp.concatenate([x] * rep, axis=1)


def _resident(shape):
    return pl.BlockSpec(shape, lambda *_: (0,) * len(shape), pipeline_mode=pl.Buffered(1))


def _x_block(x, tm):
    token_major = x.shape == (SEQ, D_MODEL)
    if token_major:
        return pl.BlockSpec((tm, D_MODEL), lambda i: (i, 0)), True
    return pl.BlockSpec((D_MODEL, tm), lambda i: (0, i)), False


def _inproj_kernel(*refs, nq, nkv, gate_w, rope, x_token_major, tm):
    if rope:
        (x_ref, ng_ref, w_ref, qg_ref, kg_ref, cos_ref, sin_ref,
         qT_ref, k_ref, vT_ref, sgT_ref) = refs
        cos, sin = cos_ref[...], sin_ref[...]
    else:
        x_ref, ng_ref, w_ref, qg_ref, kg_ref, qT_ref, k_ref, vT_ref, sgT_ref = refs
    rep = tm // LANES
    x = x_ref[...].T if x_token_major else x_ref[...]
    r = lax.rsqrt(jnp.mean(x * x, axis=0, keepdims=True) + NORM_EPS)
    h = (x * r * _tile_lanes(ng_ref[...], rep)).astype(BF16)
    qg = _tile_lanes(qg_ref[...], rep)
    kg = _tile_lanes(kg_ref[...], rep)

    def head_norm(ph, gain, scale):
        ss = jnp.sum(ph * ph, axis=0, keepdims=True)
        y = ph * (lax.rsqrt(ss * (1.0 / HEAD_DIM) + NORM_EPS) * scale) * gain
        if rope:
            half = HEAD_DIM // 2
            x1, x2 = y[:half], y[half:]
            y = jnp.concatenate([x1 * cos - x2 * sin, x1 * sin + x2 * cos], axis=0)
        return y

    qw, kw = nq * HEAD_DIM, nkv * HEAD_DIM
    chunk = 256
    for c0 in range(0, qw, chunk):
        pc = jnp.dot(w_ref[c0:c0 + chunk, :], h, preferred_element_type=F32)
        for j in range(chunk // HEAD_DIM):
            y = head_norm(pc[j * HEAD_DIM:(j + 1) * HEAD_DIM], qg, Q_SCALE)
            qT_ref[c0 + j * HEAD_DIM:c0 + (j + 1) * HEAD_DIM, :] = y.astype(BF16)
    pk = jnp.dot(w_ref[qw:qw + kw, :], h, preferred_element_type=F32)
    kn = jnp.concatenate(
        [head_norm(pk[j * HEAD_DIM:(j + 1) * HEAD_DIM], kg, 1.0) for j in range(nkv)], axis=0)
    k_ref[...] = kn.T.astype(BF16)
    pv = jnp.dot(w_ref[qw + kw:qw + 2 * kw, :], h, preferred_element_type=F32)
    vT_ref[...] = pv.astype(BF16)
    g0 = qw + 2 * kw
    for c0 in range(0, gate_w, chunk):
        pg = jnp.dot(w_ref[g0 + c0:g0 + c0 + chunk, :], h, preferred_element_type=F32)
        sgT_ref[c0:c0 + chunk, :] = (pg * (1.0 / (1.0 + jnp.exp(-pg)))).astype(BF16)


def _lane_bcast(v):
    return jnp.broadcast_to(v.astype(F32)[:, None], (v.shape[0], LANES))


def _inproj(xT, norm_gain, w_in, q_gain, k_gain, *, nq, nkv, gate_w, rope_tables=None, tm=512):
    qw, kw = nq * HEAD_DIM, nkv * HEAD_DIM
    in_w = qw + 2 * kw + gate_w
    wT = w_in.T
    rope = rope_tables is not None
    if rope:
        perm = np.concatenate([np.arange(0, HEAD_DIM, 2), np.arange(1, HEAD_DIM, 2)])
        rows = np.arange(in_w)
        nqk = nq + nkv
        rows[:nqk * HEAD_DIM] = (np.arange(nqk)[:, None] * HEAD_DIM + perm[None, :]).reshape(-1)
        wT = wT[rows]
        q_gain, k_gain = q_gain[perm], k_gain[perm]
    wT = wT.astype(BF16)
    const = lambda i: (0, 0)
    col = lambda i: (0, i)
    x_spec, x_token_major = _x_block(xT, tm)
    in_specs = [
        x_spec,
        pl.BlockSpec((D_MODEL, LANES), const),
        pl.BlockSpec((in_w, D_MODEL), const),
        pl.BlockSpec((HEAD_DIM, LANES), const),
        pl.BlockSpec((HEAD_DIM, LANES), const),
    ]
    args = [xT, _lane_bcast(norm_gain), wT, _lane_bcast(q_gain), _lane_bcast(k_gain)]
    if rope:
        in_specs += [pl.BlockSpec((HEAD_DIM // 2, tm), col)] * 2
        args += list(rope_tables)
    return pl.pallas_call(
        functools.partial(_inproj_kernel, nq=nq, nkv=nkv, gate_w=gate_w, rope=rope,
                          x_token_major=x_token_major, tm=tm),
        grid=(SEQ // tm,),
        in_specs=in_specs,
        out_specs=[
            pl.BlockSpec((qw, tm), col),
            pl.BlockSpec((tm, kw), lambda i: (i, 0)),
            pl.BlockSpec((kw, tm), col),
            pl.BlockSpec((gate_w, tm), col),
        ],
        out_shape=[
            jax.ShapeDtypeStruct((qw, SEQ), BF16),
            jax.ShapeDtypeStruct((SEQ, kw), BF16),
            jax.ShapeDtypeStruct((kw, SEQ), BF16),
            jax.ShapeDtypeStruct((gate_w, SEQ), BF16),
        ],
        compiler_params=_params("parallel"),
        name="inproj",
    )(*args)


def _fill_qpad(qT_ref, qpad_ref, head0, n_heads, tq):
    zeros = jnp.zeros((HEAD_DIM, tq), BF16)
    for h in range(n_heads):
        q = qT_ref[h * HEAD_DIM:(h + 1) * HEAD_DIM, :]
        lo, hi = (q, zeros) if ((head0 + h) // Q_PER_KV) % 2 == 0 else (zeros, q)
        qpad_ref[h, :HEAD_DIM, :] = lo
        qpad_ref[h, HEAD_DIM:, :] = hi


def _ones_rows(n):
    row = lax.broadcasted_iota(jnp.int32, (BF16_SUBLANES, n), 0)
    return jnp.where(row == 0, 1.0, 0.0).astype(BF16)


def _v_aug(vT_ref, kv, cols, ones_rows):
    return jnp.concatenate([vT_ref[kv * HEAD_DIM:(kv + 1) * HEAD_DIM, cols], ones_rows], axis=0)


def _two_stage_chunks(n_chunks, produce, consume, carry, unroll=2):
    heads = range(HEADS_PER_PAIR)
    assert unroll % 2 == 0 and n_chunks % 2 == 0
    n_tail = 2 + (n_chunks - 2) % unroll

    def stage_at(c, buf, maxes, carry, last=False):
        next_maxes, out = [], []
        for hh in heads:
            if not last:
                next_maxes.append(produce(c + 1, 1 - buf, hh))
            out.append(consume(c, buf, hh, maxes[hh], carry[hh]))
        return tuple(next_maxes), tuple(out)

    def body(jj, state):
        for u in range(unroll):
            state = stage_at(unroll * jj + u, u % 2, *state)
        return state

    state = (tuple(produce(0, 0, hh) for hh in heads), carry)
    state = lax.fori_loop(0, (n_chunks - n_tail) // unroll, body, state)
    for c in range(n_chunks - n_tail, n_chunks):
        state = stage_at(c, c % 2, *state, last=c == n_chunks - 1)
    return state[1]


def _dense_attn_kernel(qT_ref, k_ref, vT_ref, oT_ref, qpad_ref, acc_ref, p0_ref, p1_ref,
                       s0_ref, s1_ref, *, tq, tk, unroll):
    n_chunks = SEQ // tk
    p_bufs, s_bufs = (p0_ref, p1_ref), (s0_ref, s1_ref)
    _fill_qpad(qT_ref, qpad_ref, 0, HEADS_PER_PAIR, tq)
    ones_rows = _ones_rows(tk)

    def chunk_scores(c, hh):
        off = pl.multiple_of(c * tk, tk)
        return jnp.dot(k_ref[pl.ds(off, tk), :], qpad_ref[hh], preferred_element_type=F32)

    def chunk_values(c, hh, p):
        off = pl.multiple_of(c * tk, tk)
        return jnp.dot(_v_aug(vT_ref, hh // Q_PER_KV, pl.ds(off, tk), ones_rows), p,
                       preferred_element_type=F32)

    acc_ref[...] = jnp.zeros(acc_ref.shape, F32)

    def fast_produce(c, buf, hh):
        s = chunk_scores(c, hh)
        p_bufs[buf][hh] = jnp.exp2(s).astype(BF16)
        return jnp.max(s, axis=0, keepdims=True)

    def fast_consume(c, buf, hh, chunk_max, top):
        acc_ref[hh] = acc_ref[hh] + chunk_values(c, hh, p_bufs[buf][hh])
        return jnp.maximum(top, chunk_max)

    neg_inf = tuple(jnp.full((1, tq), -jnp.inf, F32) for _ in range(HEADS_PER_PAIR))
    top = _two_stage_chunks(n_chunks, fast_produce, fast_consume, neg_inf, unroll=unroll)
    unsafe = ((jnp.max(functools.reduce(jnp.maximum, top)) > EXP_LIMIT)
              | (jnp.min(functools.reduce(jnp.minimum, top)) < -EXP_LIMIT))

    @pl.when(unsafe)
    def _():
        acc_ref[...] = jnp.zeros(acc_ref.shape, F32)

        def safe_produce(c, buf, hh):
            s = chunk_scores(c, hh)
            s_bufs[buf][hh] = s
            return jnp.max(s, axis=0, keepdims=True)

        def safe_consume(c, buf, hh, chunk_max, m):
            mn = jnp.maximum(m, chunk_max)
            p = jnp.exp2(s_bufs[buf][hh] - mn).astype(BF16)
            acc_ref[hh] = jnp.exp2(m - mn) * acc_ref[hh] + chunk_values(c, hh, p)
            return mn

        _two_stage_chunks(n_chunks, safe_produce, safe_consume, neg_inf)

    for hh in range(HEADS_PER_PAIR):
        inv = 1.0 / acc_ref[hh, HEAD_DIM:HEAD_DIM + 1, :]
        oT_ref[hh * HEAD_DIM:(hh + 1) * HEAD_DIM, :] = (acc_ref[hh, :HEAD_DIM, :] * inv).astype(BF16)


def _dense_attn(qT, k_tm, vT, *, tq=256, tk=512, unroll=10):
    n_pairs = k_tm.shape[1] // LANES
    p_scratch = pltpu.VMEM((HEADS_PER_PAIR, tk, tq), BF16)
    s_scratch = pltpu.VMEM((HEADS_PER_PAIR, tk, tq), F32)
    return pl.pallas_call(
        functools.partial(_dense_attn_kernel, tq=tq, tk=tk, unroll=unroll),
        grid=(n_pairs, SEQ // tq),
        in_specs=[
            pl.BlockSpec((PAIR_ROWS, tq), lambda p, i: (p, i)),
            pl.BlockSpec((SEQ, LANES), lambda p, i: (0, p), pipeline_mode=pl.Buffered(1)),
            pl.BlockSpec((LANES, SEQ), lambda p, i: (p, 0), pipeline_mode=pl.Buffered(1)),
        ],
        out_specs=pl.BlockSpec((PAIR_ROWS, tq), lambda p, i: (p, i)),
        out_shape=jax.ShapeDtypeStruct(qT.shape, BF16),
        scratch_shapes=[
            pltpu.VMEM((HEADS_PER_PAIR, LANES, tq), BF16),
            pltpu.VMEM((HEADS_PER_PAIR, V_AUG_ROWS, tq), F32),
            p_scratch,
            p_scratch,
            s_scratch,
            s_scratch,
        ],
        compiler_params=_params("parallel", "arbitrary"),
        name="dense_attn",
    )(qT, k_tm, vT)


class _Band(NamedTuple):
    half_width: int
    dil: int

    @property
    def halo(self):
        return -(-self.half_width // LANES) * LANES


def _write_neg_dist(nd_ref, offset, band, lw, tq, hp_ref=None, bias_ref=None):
    rel = (offset + lax.broadcasted_iota(jnp.int32, (lw, tq), 0)
           - lax.broadcasted_iota(jnp.int32, (lw, tq), 1))
    arel = jnp.abs(rel)
    nd = jnp.where(arel <= band.half_width, -arel.astype(F32), -jnp.inf)
    if band.dil > 1:
        nd = jnp.where((rel & (band.dil - 1)) == 0, nd, -jnp.inf)
    nd_ref[...] = nd
    if bias_ref is not None:
        for head in range(bias_ref.shape[0]):
            bias_ref[head] = hp_ref[0, head] * nd - hp_ref[1, head]


def _banded_attn_kernel(hp_ref, qT_ref, k_ref, vT_ref, sgT_ref, w_ref, x_ref, out_ref,
                        qpad_ref, oT_ref, *scratch,
                        tq, bands, slot_heads, has_sink, depth, x_token_major, row_major_out):
    i = pl.program_id(0)
    n_heads = qT_ref.shape[0] // HEAD_DIM
    n_slots = len(slot_heads)
    _fill_qpad(qT_ref, qpad_ref, 0, n_heads, tq)
    if has_sink:
        bias_ref = scratch[3 * len(bands)]
    lws, wstarts, nd_refs, s_bufs, p_bufs, ones = [], [], [], [], [], []
    for b, band in enumerate(bands):
        nd_ref, s_ref, p_ref = scratch[3 * b:3 * b + 3]
        p_bufs.append(p_ref)
        lw = tq + 2 * band.halo
        wstart = jnp.clip(i * tq - band.halo, 0, SEQ - lw)
        offset = wstart - i * tq
        prev_offset = jnp.clip((i - 1) * tq - band.halo, 0, SEQ - lw) - (i - 1) * tq
        pl.when((i == 0) | (offset != prev_offset))(functools.partial(
            _write_neg_dist, nd_ref, offset, band, lw, tq,
            *((hp_ref, bias_ref) if has_sink else ())))
        lws.append(lw)
        wstarts.append(pl.multiple_of(wstart, LANES))
        nd_refs.append(nd_ref)
        s_bufs.append(s_ref)
        ones.append(_ones_rows(lw))

    def scores(slot):
        m = None
        for b in range(len(bands)):
            head = slot_heads[slot][b]
            pair = head // HEADS_PER_PAIR
            s = jnp.dot(k_ref[pl.ds(wstarts[b], lws[b]), pair * LANES:(pair + 1) * LANES],
                        qpad_ref[head], preferred_element_type=F32)
            s = s + hp_ref[0, head] * nd_refs[b][...]
            s_bufs[b][slot % SAFE_DEPTH] = s
            mb = jnp.max(s, axis=0, keepdims=True)
            m = mb if m is None else jnp.maximum(m, mb)
        return m

    def finish(slot, m):
        if has_sink:
            sink = hp_ref[1, slot_heads[slot][0]]
            m = jnp.maximum(m, sink)
        acc = None
        for b in range(len(bands)):
            kv = slot_heads[slot][b] // Q_PER_KV
            p = jnp.exp2(s_bufs[b][slot % SAFE_DEPTH] - m).astype(BF16)
            part = jnp.dot(_v_aug(vT_ref, kv, pl.ds(wstarts[b], lws[b]), ones[b]), p,
                           preferred_element_type=F32)
            acc = part if acc is None else acc + part
        den = acc[HEAD_DIM:HEAD_DIM + 1, :]
        if has_sink:
            den = den + jnp.exp2(sink - m)
        oT_ref[slot * HEAD_DIM:(slot + 1) * HEAD_DIM, :] = (
            acc[:HEAD_DIM, :] * (1.0 / den)).astype(BF16)

    def pipelined(produce, consume, ahead):
        produced = [produce(slot) for slot in range(ahead)]
        for slot in range(n_slots):
            if slot + ahead < n_slots:
                produced.append(produce(slot + ahead))
            consume(slot, produced[slot])
        return produced

    def fast_scores(slot):
        top = None
        for b in range(len(bands)):
            head = slot_heads[slot][b]
            pair = head // HEADS_PER_PAIR
            bias = bias_ref[head] if has_sink else hp_ref[0, head] * nd_refs[b][...]
            t = jnp.dot(k_ref[pl.ds(wstarts[b], lws[b]), pair * LANES:(pair + 1) * LANES],
                        qpad_ref[head], preferred_element_type=F32) + bias
            p_bufs[b][slot % depth] = jnp.exp2(t).astype(BF16)
            tb = jnp.max(t, axis=0, keepdims=True)
            top = tb if top is None else jnp.maximum(top, tb)
        return top

    def fast_finish(slot, _):
        acc = None
        for b in range(len(bands)):
            kv = slot_heads[slot][b] // Q_PER_KV
            part = jnp.dot(_v_aug(vT_ref, kv, pl.ds(wstarts[b], lws[b]), ones[b]),
                           p_bufs[b][slot % depth], preferred_element_type=F32)
            acc = part if acc is None else acc + part
        den = acc[HEAD_DIM:HEAD_DIM + 1, :]
        if has_sink:
            den = den + 1.0
        oT_ref[slot * HEAD_DIM:(slot + 1) * HEAD_DIM, :] = (
            acc[:HEAD_DIM, :] * (1.0 / den)).astype(BF16)

    tops = pipelined(fast_scores, fast_finish, depth - 1)
    unsafe = jnp.max(functools.reduce(jnp.maximum, tops)) > EXP_LIMIT
    if not has_sink:
        unsafe |= jnp.min(functools.reduce(jnp.minimum, tops)) < -EXP_LIMIT

    @pl.when(unsafe)
    def _():
        pipelined(scores, finish, SAFE_DEPTH - 1)

    og = oT_ref[...] * sgT_ref[...]
    x = x_ref[...].T if x_token_major else x_ref[...]
    y = x + jnp.dot(w_ref[...], og, preferred_element_type=F32)
    out_ref[...] = y.T if row_major_out else y


def _residual_out(row_major_out, tm):
    if row_major_out:
        return (pl.BlockSpec((tm, D_MODEL), lambda i: (i, 0)),
                jax.ShapeDtypeStruct((SEQ, D_MODEL), F32))
    return (pl.BlockSpec((D_MODEL, tm), lambda i: (0, i)),
            jax.ShapeDtypeStruct((D_MODEL, SEQ), F32))


def _banded_mixer(head_params, qT, k_tm, vT, sgT, w_out, xT, *, bands, slot_heads, has_sink,
                  depth, row_major_out=False, tq=256):
    n_heads = qT.shape[0] // HEAD_DIM
    width = len(slot_heads) * HEAD_DIM
    col = lambda i: (0, i)
    scratch = [pltpu.VMEM((n_heads, LANES, tq), BF16), pltpu.VMEM((width, tq), BF16)]
    for band in bands:
        lw = tq + 2 * band.halo
        scratch += [pltpu.VMEM((lw, tq), F32), pltpu.VMEM((SAFE_DEPTH, lw, tq), F32),
                    pltpu.VMEM((depth, lw, tq), BF16)]
    if has_sink:
        assert len(bands) == 1, "the per-head bias cache is sized for one band"
        scratch += [pltpu.VMEM((n_heads, lw, tq), F32)]
    x_spec, x_token_major = _x_block(xT, tq)
    out_spec, out_shape = _residual_out(row_major_out, tq)
    woT = w_out.T.astype(BF16)
    return pl.pallas_call(
        functools.partial(_banded_attn_kernel, tq=tq, bands=bands, slot_heads=slot_heads,
                          has_sink=has_sink, depth=depth, x_token_major=x_token_major,
                          row_major_out=row_major_out),
        grid=(SEQ // tq,),
        in_specs=[
            pl.BlockSpec(memory_space=pltpu.SMEM),
            pl.BlockSpec((qT.shape[0], tq), col),
            _resident(k_tm.shape),
            _resident(vT.shape),
            pl.BlockSpec((width, tq), col),
            _resident(woT.shape),
            x_spec,
        ],
        out_specs=out_spec,
        out_shape=out_shape,
        scratch_shapes=scratch,
        compiler_params=_params("arbitrary"),
        name="banded_mixer",
    )(head_params, qT, k_tm, vT, sgT, woT, xT)


def _outproj_kernel(oT_ref, sgT_ref, w_ref, x_ref, out_ref, *, x_token_major, row_major_out):
    og = oT_ref[...] * sgT_ref[...]
    x = x_ref[...].T if x_token_major else x_ref[...]
    y = x + jnp.dot(w_ref[...], og, preferred_element_type=F32)
    out_ref[...] = y.T if row_major_out else y


def _outproj(oT, sgT, w_out, xT, *, row_major_out=False, tm=512):
    width = sgT.shape[0]
    woT = w_out.T.astype(BF16)
    col = lambda i: (0, i)
    x_spec, x_token_major = _x_block(xT, tm)
    out_spec, out_shape = _residual_out(row_major_out, tm)
    return pl.pallas_call(
        functools.partial(_outproj_kernel, x_token_major=x_token_major,
                          row_major_out=row_major_out),
        grid=(SEQ // tm,),
        in_specs=[
            pl.BlockSpec((width, tm), col),
            pl.BlockSpec((width, tm), col),
            pl.BlockSpec((D_MODEL, width), lambda i: (0, 0)),
            x_spec,
        ],
        out_specs=out_spec,
        out_shape=out_shape,
        compiler_params=_params("parallel"),
        name="outproj",
    )(oT, sgT, woT, xT)


def _alibi_slopes(n):
    return jnp.asarray(2.0 ** (-8.0 * np.arange(1, n + 1) / n), dtype=F32)


def _rope_tables():
    t = np.arange(SEQ)
    axis_dim = HEAD_DIM // 2
    freqs = (1.0 / (np.float32(ROPE_THETA) ** (np.arange(0, axis_dim, 2, dtype=np.float32)
                                                / np.float32(axis_dim)))).astype(np.float32)
    row = (t // GRID_W).astype(np.float32)
    col = (t % GRID_W).astype(np.float32)
    ang = np.concatenate([freqs[:, None] * row[None, :], freqs[:, None] * col[None, :]], axis=0)
    ang = ang.astype(np.float32).astype(np.float64)
    return jnp.asarray(np.cos(ang), F32), jnp.asarray(np.sin(ang), F32)


def _mixer_a(xT, norm, w_in, q_gain, k_gain, sink, w_out, *, last):
    qT, k_tm, vT, sgT = _inproj(xT, norm, w_in, q_gain, k_gain, nq=16, nkv=4, gate_w=1024)
    hp = jnp.stack([_alibi_slopes(16), sink.astype(F32)]) * LOG2E
    return _banded_mixer(hp, qT, k_tm, vT, sgT, w_out, xT, bands=(_Band(A_WINDOW, 1),),
                         slot_heads=tuple((h,) for h in range(16)), has_sink=True, depth=6,
                         row_major_out=last)


def _mixer_b(xT, norm, w_in, q_gain, k_gain, w_out):
    qT, k_tm, vT, sgT = _inproj(xT, norm, w_in, q_gain, k_gain, nq=16, nkv=4, gate_w=1024,
                                rope_tables=_rope_tables())
    oT = _dense_attn(qT, k_tm, vT)
    return _outproj(oT, sgT, w_out, xT)


def _mixer_c(xT, norm, w_in, q_gain, k_gain, w_out):
    qT, k_tm, vT, sgT = _inproj(xT, norm, w_in, q_gain, k_gain, nq=24, nkv=6, gate_w=512)
    slopes = _alibi_slopes(24) * LOG2E
    hp = jnp.stack([slopes, jnp.zeros_like(slopes)])
    bands = tuple(_Band(window // 2, dil) for window, dil in C_GROUPS)
    n_slots = HEADS_PER_PAIR
    slot_heads = tuple(tuple(g * n_slots + s for g in range(len(bands))) for s in range(n_slots))
    return _banded_mixer(hp, qT, k_tm, vT, sgT, w_out, xT, bands=bands, slot_heads=slot_heads,
                         has_sink=False, depth=4)


def kernel(x, l0_norm, l0_w_in, l0_q_gain, l0_k_gain, l0_sink, l0_w_out,
           l1_norm, l1_w_in, l1_q_gain, l1_k_gain, l1_w_out,
           l2_norm, l2_w_in, l2_q_gain, l2_k_gain, l2_w_out,
           l3_norm, l3_w_in, l3_q_gain, l3_k_gain, l3_sink, l3_w_out):
    xT = _mixer_a(x.reshape(SEQ, D_MODEL), l0_norm, l0_w_in, l0_q_gain, l0_k_gain, l0_sink,
                  l0_w_out, last=False)
    xT = _mixer_b(xT, l1_norm, l1_w_in, l1_q_gain, l1_k_gain, l1_w_out)
    xT = _mixer_c(xT, l2_norm, l2_w_in, l2_q_gain, l2_k_gain, l2_w_out)
    out = _mixer_a(xT, l3_norm, l3_w_in, l3_q_gain, l3_k_gain, l3_sink, l3_w_out, last=True)
    return out.reshape(x.shape)
```

```python
import functools
from typing import NamedTuple

import numpy as np
import jax
import jax.numpy as jnp
from jax import lax
from jax.experimental import pallas as pl
from jax.experimental.pallas import tpu as pltpu

D_MODEL = 1024
SEQ = 16384
HEAD_DIM = 64
NORM_EPS = 1e-6
GRID_W = 64
ROPE_THETA = 10000.0
A_WINDOW = 128
C_GROUPS = ((128, 1), (512, 4), (2048, 16))
LOG2E = float(np.log2(np.e))
Q_SCALE = HEAD_DIM ** -0.5 * LOG2E

LANES = 128
BF16_SUBLANES = 16
Q_PER_KV = 4
HEADS_PER_PAIR = 2 * Q_PER_KV
PAIR_ROWS = HEADS_PER_PAIR * HEAD_DIM
V_AUG_ROWS = HEAD_DIM + BF16_SUBLANES
EXP_LIMIT = 64.0
SAFE_DEPTH = 2
VMEM_LIMIT = 56 * 1024 * 1024

BF16 = jnp.bfloat16
F32 = jnp.float32


def _params(*sem):
    return pltpu.CompilerParams(dimension_semantics=sem, vmem_limit_bytes=VMEM_LIMIT)


def _tile_lanes(x, rep):
    return x if rep == 1 else jnp.concatenate([x] * rep, axis=1)


def _resident(shape):
    return pl.BlockSpec(shape, lambda *_: (0,) * len(shape), pipeline_mode=pl.Buffered(1))


def _x_block(x, tm):
    token_major = x.shape == (SEQ, D_MODEL)
    if token_major:
        return pl.BlockSpec((tm, D_MODEL), lambda i: (i, 0)), True
    return pl.BlockSpec((D_MODEL, tm), lambda i: (0, i)), False


def _inproj_kernel(*refs, nq, nkv, gate_w, rope, x_token_major, tm):
    if rope:
        (x_ref, ng_ref, w_ref, qg_ref, kg_ref, cos_ref, sin_ref,
         qT_ref, k_ref, vT_ref, sgT_ref) = refs
        cos, sin = cos_ref[...], sin_ref[...]
    else:
        x_ref, ng_ref, w_ref, qg_ref, kg_ref, qT_ref, k_ref, vT_ref, sgT_ref = refs
    rep = tm // LANES
    x = x_ref[...].T if x_token_major else x_ref[...]
    r = lax.rsqrt(jnp.mean(x * x, axis=0, keepdims=True) + NORM_EPS)
    h = (x * r * _tile_lanes(ng_ref[...], rep)).astype(BF16)
    qg = _tile_lanes(qg_ref[...], rep)
    kg = _tile_lanes(kg_ref[...], rep)

    def head_norm(ph, gain, scale):
        ss = jnp.sum(ph * ph, axis=0, keepdims=True)
        y = ph * (lax.rsqrt(ss * (1.0 / HEAD_DIM) + NORM_EPS) * scale) * gain
        if rope:
            half = HEAD_DIM // 2
            x1, x2 = y[:half], y[half:]
            y = jnp.concatenate([x1 * cos - x2 * sin, x1 * sin + x2 * cos], axis=0)
        return y

    qw, kw = nq * HEAD_DIM, nkv * HEAD_DIM
    chunk = 256
    for c0 in range(0, qw, chunk):
        pc = jnp.dot(w_ref[c0:c0 + chunk, :], h, preferred_element_type=F32)
        for j in range(chunk // HEAD_DIM):
            y = head_norm(pc[j * HEAD_DIM:(j + 1) * HEAD_DIM], qg, Q_SCALE)
            qT_ref[c0 + j * HEAD_DIM:c0 + (j + 1) * HEAD_DIM, :] = y.astype(BF16)
    pk = jnp.dot(w_ref[qw:qw + kw, :], h, preferred_element_type=F32)
    kn = jnp.concatenate(
        [head_norm(pk[j * HEAD_DIM:(j + 1) * HEAD_DIM], kg, 1.0) for j in range(nkv)], axis=0)
    k_ref[...] = kn.T.astype(BF16)
    pv = jnp.dot(w_ref[qw + kw:qw + 2 * kw, :], h, preferred_element_type=F32)
    vT_ref[...] = pv.astype(BF16)
    g0 = qw + 2 * kw
    for c0 in range(0, gate_w, chunk):
        pg = jnp.dot(w_ref[g0 + c0:g0 + c0 + chunk, :], h, preferred_element_type=F32)
        sgT_ref[c0:c0 + chunk, :] = (pg * (1.0 / (1.0 + jnp.exp(-pg)))).astype(BF16)


def _lane_bcast(v):
    return jnp.broadcast_to(v.astype(F32)[:, None], (v.shape[0], LANES))


def _inproj(xT, norm_gain, w_in, q_gain, k_gain, *, nq, nkv, gate_w, rope_tables=None, tm=512):
    qw, kw = nq * HEAD_DIM, nkv * HEAD_DIM
    in_w = qw + 2 * kw + gate_w
    wT = w_in.T
    rope = rope_tables is not None
    if rope:
        perm = np.concatenate([np.arange(0, HEAD_DIM, 2), np.arange(1, HEAD_DIM, 2)])
        rows = np.arange(in_w)
        nqk = nq + nkv
        rows[:nqk * HEAD_DIM] = (np.arange(nqk)[:, None] * HEAD_DIM + perm[None, :]).reshape(-1)
        wT = wT[rows]
        q_gain, k_gain = q_gain[perm], k_gain[perm]
    wT = wT.astype(BF16)
    const = lambda i: (0, 0)
    col = lambda i: (0, i)
    x_spec, x_token_major = _x_block(xT, tm)
    in_specs = [
        x_spec,
        pl.BlockSpec((D_MODEL, LANES), const),
        pl.BlockSpec((in_w, D_MODEL), const),
        pl.BlockSpec((HEAD_DIM, LANES), const),
        pl.BlockSpec((HEAD_DIM, LANES), const),
    ]
    args = [xT, _lane_bcast(norm_gain), wT, _lane_bcast(q_gain), _lane_bcast(k_gain)]
    if rope:
        in_specs += [pl.BlockSpec((HEAD_DIM // 2, tm), col)] * 2
        args += list(rope_tables)
    return pl.pallas_call(
        functools.partial(_inproj_kernel, nq=nq, nkv=nkv, gate_w=gate_w, rope=rope,
                          x_token_major=x_token_major, tm=tm),
        grid=(SEQ // tm,),
        in_specs=in_specs,
        out_specs=[
            pl.BlockSpec((qw, tm), col),
            pl.BlockSpec((tm, kw), lambda i: (i, 0)),
            pl.BlockSpec((kw, tm), col),
            pl.BlockSpec((gate_w, tm), col),
        ],
        out_shape=[
            jax.ShapeDtypeStruct((qw, SEQ), BF16),
            jax.ShapeDtypeStruct((SEQ, kw), BF16),
            jax.ShapeDtypeStruct((kw, SEQ), BF16),
            jax.ShapeDtypeStruct((gate_w, SEQ), BF16),
        ],
        compiler_params=_params("parallel"),
        name="inproj",
    )(*args)


def _fill_qpad(qT_ref, qpad_ref, head0, n_heads, tq):
    zeros = jnp.zeros((HEAD_DIM, tq), BF16)
    for h in range(n_heads):
        q = qT_ref[h * HEAD_DIM:(h + 1) * HEAD_DIM, :]
        lo, hi = (q, zeros) if ((head0 + h) // Q_PER_KV) % 2 == 0 else (zeros, q)
        qpad_ref[h, :HEAD_DIM, :] = lo
        qpad_ref[h, HEAD_DIM:, :] = hi


def _ones_rows(n):
    row = lax.broadcasted_iota(jnp.int32, (BF16_SUBLANES, n), 0)
    return jnp.where(row == 0, 1.0, 0.0).astype(BF16)


def _v_aug(vT_ref, kv, cols, ones_rows):
    return jnp.concatenate([vT_ref[kv * HEAD_DIM:(kv + 1) * HEAD_DIM, cols], ones_rows], axis=0)


def _two_stage_chunks(n_chunks, produce, consume, carry, unroll=2):
    heads = range(HEADS_PER_PAIR)
    assert unroll % 2 == 0 and n_chunks % 2 == 0
    n_tail = 2 + (n_chunks - 2) % unroll

    def stage_at(c, buf, maxes, carry, last=False):
        next_maxes, out = [], []
        for hh in heads:
            if not last:
                next_maxes.append(produce(c + 1, 1 - buf, hh))
            out.append(consume(c, buf, hh, maxes[hh], carry[hh]))
        return tuple(next_maxes), tuple(out)

    def body(jj, state):
        for u in range(unroll):
            state = stage_at(unroll * jj + u, u % 2, *state)
        return state

    state = (tuple(produce(0, 0, hh) for hh in heads), carry)
    state = lax.fori_loop(0, (n_chunks - n_tail) // unroll, body, state)
    for c in range(n_chunks - n_tail, n_chunks):
        state = stage_at(c, c % 2, *state, last=c == n_chunks - 1)
    return state[1]


def _dense_attn_kernel(qT_ref, k_ref, vT_ref, oT_ref, qpad_ref, acc_ref, p0_ref, p1_ref,
                       s0_ref, s1_ref, *, tq, tk, unroll):
    n_chunks = SEQ // tk
    p_bufs, s_bufs = (p0_ref, p1_ref), (s0_ref, s1_ref)
    _fill_qpad(qT_ref, qpad_ref, 0, HEADS_PER_PAIR, tq)
    ones_rows = _ones_rows(tk)

    def chunk_scores(c, hh):
        off = pl.multiple_of(c * tk, tk)
        return jnp.dot(k_ref[pl.ds(off, tk), :], qpad_ref[hh], preferred_element_type=F32)

    def chunk_values(c, hh, p):
        off = pl.multiple_of(c * tk, tk)
        return jnp.dot(_v_aug(vT_ref, hh // Q_PER_KV, pl.ds(off, tk), ones_rows), p,
                       preferred_element_type=F32)

    acc_ref[...] = jnp.zeros(acc_ref.shape, F32)

    def fast_produce(c, buf, hh):
        s = chunk_scores(c, hh)
        p_bufs[buf][hh] = jnp.exp2(s).astype(BF16)
        return jnp.max(s, axis=0, keepdims=True)

    def fast_consume(c, buf, hh, chunk_max, top):
        acc_ref[hh] = acc_ref[hh] + chunk_values(c, hh, p_bufs[buf][hh])
        return jnp.maximum(top, chunk_max)

    neg_inf = tuple(jnp.full((1, tq), -jnp.inf, F32) for _ in range(HEADS_PER_PAIR))
    top = _two_stage_chunks(n_chunks, fast_produce, fast_consume, neg_inf, unroll=unroll)
    unsafe = ((jnp.max(functools.reduce(jnp.maximum, top)) > EXP_LIMIT)
              | (jnp.min(functools.reduce(jnp.minimum, top)) < -EXP_LIMIT))

    @pl.when(unsafe)
    def _():
        acc_ref[...] = jnp.zeros(acc_ref.shape, F32)

        def safe_produce(c, buf, hh):
            s = chunk_scores(c, hh)
            s_bufs[buf][hh] = s
            return jnp.max(s, axis=0, keepdims=True)

        def safe_consume(c, buf, hh, chunk_max, m):
            mn = jnp.maximum(m, chunk_max)
            p = jnp.exp2(s_bufs[buf][hh] - mn).astype(BF16)
            acc_ref[hh] = jnp.exp2(m - mn) * acc_ref[hh] + chunk_values(c, hh, p)
            return mn

        _two_stage_chunks(n_chunks, safe_produce, safe_consume, neg_inf)

    for hh in range(HEADS_PER_PAIR):
        inv = 1.0 / acc_ref[hh, HEAD_DIM:HEAD_DIM + 1, :]
        oT_ref[hh * HEAD_DIM:(hh + 1) * HEAD_DIM, :] = (acc_ref[hh, :HEAD_DIM, :] * inv).astype(BF16)


def _dense_attn(qT, k_tm, vT, *, tq=256, tk=512, unroll=14):
    n_pairs = k_tm.shape[1] // LANES
    p_scratch = pltpu.VMEM((HEADS_PER_PAIR, tk, tq), BF16)
    s_scratch = pltpu.VMEM((HEADS_PER_PAIR, tk, tq), F32)
    return pl.pallas_call(
        functools.partial(_dense_attn_kernel, tq=tq, tk=tk, unroll=unroll),
        grid=(n_pairs, SEQ // tq),
        in_specs=[
            pl.BlockSpec((PAIR_ROWS, tq), lambda p, i: (p, i)),
            pl.BlockSpec((SEQ, LANES), lambda p, i: (0, p), pipeline_mode=pl.Buffered(1)),
            pl.BlockSpec((LANES, SEQ), lambda p, i: (p, 0), pipeline_mode=pl.Buffered(1)),
        ],
        out_specs=pl.BlockSpec((PAIR_ROWS, tq), lambda p, i: (p, i)),
        out_shape=jax.ShapeDtypeStruct(qT.shape, BF16),
        scratch_shapes=[
            pltpu.VMEM((HEADS_PER_PAIR, LANES, tq), BF16),
            pltpu.VMEM((HEADS_PER_PAIR, V_AUG_ROWS, tq), F32),
            p_scratch,
            p_scratch,
            s_scratch,
            s_scratch,
        ],
        compiler_params=_params("parallel", "arbitrary"),
        name="dense_attn",
    )(qT, k_tm, vT)


class _Band(NamedTuple):
    half_width: int
    dil: int

    @property
    def halo(self):
        return -(-self.half_width // LANES) * LANES


def _write_neg_dist(nd_ref, offset, band, lw, tq, hp_ref=None, bias_ref=None):
    rel = (offset + lax.broadcasted_iota(jnp.int32, (lw, tq), 0)
           - lax.broadcasted_iota(jnp.int32, (lw, tq), 1))
    arel = jnp.abs(rel)
    nd = jnp.where(arel <= band.half_width, -arel.astype(F32), -jnp.inf)
    if band.dil > 1:
        nd = jnp.where((rel & (band.dil - 1)) == 0, nd, -jnp.inf)
    nd_ref[...] = nd
    if bias_ref is not None:
        for head in range(bias_ref.shape[0]):
            bias_ref[head] = hp_ref[0, head] * nd - hp_ref[1, head]


def _banded_attn_kernel(hp_ref, qT_ref, k_ref, vT_ref, sgT_ref, w_ref, x_ref, out_ref,
                        qpad_ref, oT_ref, *scratch,
                        tq, bands, slot_heads, has_sink, depth, x_token_major, row_major_out):
    i = pl.program_id(0)
    n_heads = qT_ref.shape[0] // HEAD_DIM
    n_slots = len(slot_heads)
    _fill_qpad(qT_ref, qpad_ref, 0, n_heads, tq)
    if has_sink:
        bias_ref = scratch[3 * len(bands)]
    lws, wstarts, nd_refs, s_bufs, p_bufs, ones = [], [], [], [], [], []
    for b, band in enumerate(bands):
        nd_ref, s_ref, p_ref = scratch[3 * b:3 * b + 3]
        p_bufs.append(p_ref)
        lw = tq + 2 * band.halo
        wstart = jnp.clip(i * tq - band.halo, 0, SEQ - lw)
        offset = wstart - i * tq
        prev_offset = jnp.clip((i - 1) * tq - band.halo, 0, SEQ - lw) - (i - 1) * tq
        pl.when((i == 0) | (offset != prev_offset))(functools.partial(
            _write_neg_dist, nd_ref, offset, band, lw, tq,
            *((hp_ref, bias_ref) if has_sink else ())))
        lws.append(lw)
        wstarts.append(pl.multiple_of(wstart, LANES))
        nd_refs.append(nd_ref)
        s_bufs.append(s_ref)
        ones.append(_ones_rows(lw))

    def scores(slot):
        m = None
        for b in range(len(bands)):
            head = slot_heads[slot][b]
            pair = head // HEADS_PER_PAIR
            s = jnp.dot(k_ref[pl.ds(wstarts[b], lws[b]), pair * LANES:(pair + 1) * LANES],
                        qpad_ref[head], preferred_element_type=F32)
            s = s + hp_ref[0, head] * nd_refs[b][...]
            s_bufs[b][slot % SAFE_DEPTH] = s
            mb = jnp.max(s, axis=0, keepdims=True)
            m = mb if m is None else jnp.maximum(m, mb)
        return m

    def finish(slot, m):
        if has_sink:
            sink = hp_ref[1, slot_heads[slot][0]]
            m = jnp.maximum(m, sink)
        acc = None
        for b in range(len(bands)):
            kv = slot_heads[slot][b] // Q_PER_KV
            p = jnp.exp2(s_bufs[b][slot % SAFE_DEPTH] - m).astype(BF16)
            part = jnp.dot(_v_aug(vT_ref, kv, pl.ds(wstarts[b], lws[b]), ones[b]), p,
                           preferred_element_type=F32)
            acc = part if acc is None else acc + part
        den = acc[HEAD_DIM:HEAD_DIM + 1, :]
        if has_sink:
            den = den + jnp.exp2(sink - m)
        oT_ref[slot * HEAD_DIM:(slot + 1) * HEAD_DIM, :] = (
            acc[:HEAD_DIM, :] * (1.0 / den)).astype(BF16)

    def pipelined(produce, consume, ahead):
        produced = [produce(slot) for slot in range(ahead)]
        for slot in range(n_slots):
            if slot + ahead < n_slots:
                produced.append(produce(slot + ahead))
            consume(slot, produced[slot])
        return produced

    def fast_scores(slot):
        top = None
        for b in range(len(bands)):
            head = slot_heads[slot][b]
            pair = head // HEADS_PER_PAIR
            bias = bias_ref[head] if has_sink else hp_ref[0, head] * nd_refs[b][...]
            t = jnp.dot(k_ref[pl.ds(wstarts[b], lws[b]), pair * LANES:(pair + 1) * LANES],
                        qpad_ref[head], preferred_element_type=F32) + bias
            p_bufs[b][slot % depth] = jnp.exp2(t).astype(BF16)
            tb = jnp.max(t, axis=0, keepdims=True)
            top = tb if top is None else jnp.maximum(top, tb)
        return top

    def fast_finish(slot, _):
        acc = None
        for b in range(len(bands)):
            kv = slot_heads[slot][b] // Q_PER_KV
            part = jnp.dot(_v_aug(vT_ref, kv, pl.ds(wstarts[b], lws[b]), ones[b]),
                           p_bufs[b][slot % depth], preferred_element_type=F32)
            acc = part if acc is None else acc + part
        den = acc[HEAD_DIM:HEAD_DIM + 1, :]
        if has_sink:
            den = den + 1.0
        oT_ref[slot * HEAD_DIM:(slot + 1) * HEAD_DIM, :] = (
            acc[:HEAD_DIM, :] * (1.0 / den)).astype(BF16)

    tops = pipelined(fast_scores, fast_finish, depth - 1)
    unsafe = jnp.max(functools.reduce(jnp.maximum, tops)) > EXP_LIMIT
    if not has_sink:
        unsafe |= jnp.min(functools.reduce(jnp.minimum, tops)) < -EXP_LIMIT

    @pl.when(unsafe)
    def _():
        pipelined(scores, finish, SAFE_DEPTH - 1)

    og = oT_ref[...] * sgT_ref[...]
    x = x_ref[...].T if x_token_major else x_ref[...]
    y = x + jnp.dot(w_ref[...], og, preferred_element_type=F32)
    out_ref[...] = y.T if row_major_out else y


def _residual_out(row_major_out, tm):
    if row_major_out:
        return (pl.BlockSpec((tm, D_MODEL), lambda i: (i, 0)),
                jax.ShapeDtypeStruct((SEQ, D_MODEL), F32))
    return (pl.BlockSpec((D_MODEL, tm), lambda i: (0, i)),
            jax.ShapeDtypeStruct((D_MODEL, SEQ), F32))


def _banded_mixer(head_params, qT, k_tm, vT, sgT, w_out, xT, *, bands, slot_heads, has_sink,
                  depth, row_major_out=False, tq=256):
    n_heads = qT.shape[0] // HEAD_DIM
    width = len(slot_heads) * HEAD_DIM
    col = lambda i: (0, i)
    scratch = [pltpu.VMEM((n_heads, LANES, tq), BF16), pltpu.VMEM((width, tq), BF16)]
    for band in bands:
        lw = tq + 2 * band.halo
        scratch += [pltpu.VMEM((lw, tq), F32), pltpu.VMEM((SAFE_DEPTH, lw, tq), F32),
                    pltpu.VMEM((depth, lw, tq), BF16)]
    if has_sink:
        assert len(bands) == 1, "the per-head bias cache is sized for one band"
        scratch += [pltpu.VMEM((n_heads, lw, tq), F32)]
    x_spec, x_token_major = _x_block(xT, tq)
    out_spec, out_shape = _residual_out(row_major_out, tq)
    woT = w_out.T.astype(BF16)
    return pl.pallas_call(
        functools.partial(_banded_attn_kernel, tq=tq, bands=bands, slot_heads=slot_heads,
                          has_sink=has_sink, depth=depth, x_token_major=x_token_major,
                          row_major_out=row_major_out),
        grid=(SEQ // tq,),
        in_specs=[
            pl.BlockSpec(memory_space=pltpu.SMEM),
            pl.BlockSpec((qT.shape[0], tq), col),
            _resident(k_tm.shape),
            _resident(vT.shape),
            pl.BlockSpec((width, tq), col),
            _resident(woT.shape),
            x_spec,
        ],
        out_specs=out_spec,
        out_shape=out_shape,
        scratch_shapes=scratch,
        compiler_params=_params("arbitrary"),
        name="banded_mixer",
    )(head_params, qT, k_tm, vT, sgT, woT, xT)


def _outproj_kernel(oT_ref, sgT_ref, w_ref, x_ref, out_ref, *, x_token_major, row_major_out):
    og = oT_ref[...] * sgT_ref[...]
    x = x_ref[...].T if x_token_major else x_ref[...]
    y = x + jnp.dot(w_ref[...], og, preferred_element_type=F32)
    out_ref[...] = y.T if row_major_out else y


def _outproj(oT, sgT, w_out, xT, *, row_major_out=False, tm=512):
    width = sgT.shape[0]
    woT = w_out.T.astype(BF16)
    col = lambda i: (0, i)
    x_spec, x_token_major = _x_block(xT, tm)
    out_spec, out_shape = _residual_out(row_major_out, tm)
    return pl.pallas_call(
        functools.partial(_outproj_kernel, x_token_major=x_token_major,
                          row_major_out=row_major_out),
        grid=(SEQ // tm,),
        in_specs=[
            pl.BlockSpec((width, tm), col),
            pl.BlockSpec((width, tm), col),
            pl.BlockSpec((D_MODEL, width), lambda i: (0, 0)),
            x_spec,
        ],
        out_specs=out_spec,
        out_shape=out_shape,
        compiler_params=_params("parallel"),
        name="outproj",
    )(oT, sgT, woT, xT)


def _alibi_slopes(n):
    return jnp.asarray(2.0 ** (-8.0 * np.arange(1, n + 1) / n), dtype=F32)


def _rope_tables():
    t = np.arange(SEQ)
    axis_dim = HEAD_DIM // 2
    freqs = (1.0 / (np.float32(ROPE_THETA) ** (np.arange(0, axis_dim, 2, dtype=np.float32)
                                                / np.float32(axis_dim)))).astype(np.float32)
    row = (t // GRID_W).astype(np.float32)
    col = (t % GRID_W).astype(np.float32)
    ang = np.concatenate([freqs[:, None] * row[None, :], freqs[:, None] * col[None, :]], axis=0)
    ang = ang.astype(np.float32).astype(np.float64)
    return jnp.asarray(np.cos(ang), F32), jnp.asarray(np.sin(ang), F32)


def _mixer_a(xT, norm, w_in, q_gain, k_gain, sink, w_out, *, last):
    qT, k_tm, vT, sgT = _inproj(xT, norm, w_in, q_gain, k_gain, nq=16, nkv=4, gate_w=1024)
    hp = jnp.stack([_alibi_slopes(16), sink.astype(F32)]) * LOG2E
    return _banded_mixer(hp, qT, k_tm, vT, sgT, w_out, xT, bands=(_Band(A_WINDOW, 1),),
                         slot_heads=tuple((h,) for h in range(16)), has_sink=True, depth=6,
                         row_major_out=last)


def _mixer_b(xT, norm, w_in, q_gain, k_gain, w_out):
    qT, k_tm, vT, sgT = _inproj(xT, norm, w_in, q_gain, k_gain, nq=16, nkv=4, gate_w=1024,
                                rope_tables=_rope_tables())
    oT = _dense_attn(qT, k_tm, vT)
    return _outproj(oT, sgT, w_out, xT)


def _mixer_c(xT, norm, w_in, q_gain, k_gain, w_out):
    qT, k_tm, vT, sgT = _inproj(xT, norm, w_in, q_gain, k_gain, nq=24, nkv=6, gate_w=512)
    slopes = _alibi_slopes(24) * LOG2E
    hp = jnp.stack([slopes, jnp.zeros_like(slopes)])
    bands = tuple(_Band(window // 2, dil) for window, dil in C_GROUPS)
    n_slots = HEADS_PER_PAIR
    slot_heads = tuple(tuple(g * n_slots + s for g in range(len(bands))) for s in range(n_slots))
    return _banded_mixer(hp, qT, k_tm, vT, sgT, w_out, xT, bands=bands, slot_heads=slot_heads,
                         has_sink=False, depth=4)


def kernel(x, l0_norm, l0_w_in, l0_q_gain, l0_k_gain, l0_sink, l0_w_out,
           l1_norm, l1_w_in, l1_q_gain, l1_k_gain, l1_w_out,
           l2_norm, l2_w_in, l2_q_gain, l2_k_gain, l2_w_out,
           l3_norm, l3_w_in, l3_q_gain, l3_k_gain, l3_sink, l3_w_out):
    xT = _mixer_a(x.reshape(SEQ, D_MODEL), l0_norm, l0_w_in, l0_q_gain, l0_k_gain, l0_sink,
                  l0_w_out, last=False)
    xT = _mixer_b(xT, l1_norm, l1_w_in, l1_q_gain, l1_k_gain, l1_w_out)
    xT = _mixer_c(xT, l2_norm, l2_w_in, l2_q_gain, l2_k_gain, l2_w_out)
    out = _mixer_a(xT, l3_norm, l3_w_in, l3_q_gain, l3_k_gain, l3_sink, l3_w_out, last=True)
    return out.reshape(x.shape)
```
